```python
import math
import jax, jax.numpy as jnp
from jax import lax
import numpy as np

D_MODEL = 2048
BATCH = 1
SEQ = 16384
DEPTH = 4

EPS = 1e-6
MLA_HEADS = 8
MLA_NOPE = 128
MLA_ROPE = 64
MLA_V = 128
MLA_Q_RANK = 512
MLA_KV_RANK = 512
ROPE_THETA = 10000.0
Q_BLOCK = 128
GLA_HEADS = 4
GLA_DK = 128
GLA_DV = 256
GLA_GATE_RANK = 16
GLA_GATE_NORM = 16.0
GDN_QK_HEADS = 4
GDN_V_HEADS = 8
GDN_DK = 128
GDN_DV = 128
CONV_WIDTH = 4
CHUNK = 64
D_FF = 5632
FFN_CONV = 3
N_BRANCH = 3

MLA_W = MLA_HEADS * MLA_V
GLA_W = GLA_HEADS * GLA_DV
GDN_W = GDN_V_HEADS * GDN_DV
GLA_KW = GLA_HEADS * GLA_DK
GDN_KW = GDN_QK_HEADS * GDN_DK
GDN_CONV_DIM = 2 * GDN_KW + GDN_W
IN_SPLITS = (MLA_Q_RANK, MLA_KV_RANK, MLA_ROPE,
             GLA_KW, GLA_KW, GLA_W, GLA_GATE_RANK, GLA_W,
             GDN_CONV_DIM, GDN_V_HEADS, GDN_V_HEADS, GDN_W,
             N_BRANCH * D_MODEL)
IN_DIM = sum(IN_SPLITS)

kernel_name = "hybrid_mla_gla_gdn_convffn"


def _split_points():
    pts, acc = [], 0
    for w in IN_SPLITS[:-1]:
        acc += w
        pts.append(acc)
    return pts


def rms_norm(x, g):
    xf = x.astype(jnp.float32)
    y = xf * lax.rsqrt(jnp.mean(xf * xf, axis=-1, keepdims=True) + EPS)
    return (y * g.astype(jnp.float32)).astype(x.dtype)


def l2_normalize(x):
    xf = x.astype(jnp.float32)
    return xf * lax.rsqrt(jnp.sum(xf * xf, axis=-1, keepdims=True) + EPS)


def causal_dwconv(x, w):
    width, ch = w.shape
    return lax.conv_general_dilated(x, w[:, None, :].astype(x.dtype), window_strides=(1,),
                                    padding=[(width - 1, 0)],
                                    dimension_numbers=('NWC', 'WIO', 'NWC'),
                                    feature_group_count=ch)


def rope_tables(positions):
    inv = ROPE_THETA ** (-jnp.arange(0, MLA_ROPE, 2, dtype=jnp.float32) / MLA_ROPE)
    ang = positions.astype(jnp.float32)[..., None] * inv
    return jnp.cos(ang), jnp.sin(ang)


def apply_rope(x, cos, sin):
    x1, x2 = jnp.split(x, 2, axis=-1)
    return jnp.concatenate([x1 * cos - x2 * sin, x1 * sin + x2 * cos], axis=-1).astype(x.dtype)


def to_chunks(t):
    b, s, h, d = t.shape
    return t.reshape(b, s // CHUNK, CHUNK, h, d).transpose(1, 0, 3, 2, 4)


def from_chunks(t):
    n, b, h, c, d = t.shape
    return t.transpose(1, 0, 3, 2, 4).reshape(b, n * c, h, d)


def mla_branch(c_q, c_kv, k_rope, cos, sin, q_norm, kv_norm, w_uq, w_ukv):
    B, S, _ = c_q.shape
    q = (rms_norm(c_q, q_norm) @ w_uq).reshape(B, S, MLA_HEADS, MLA_NOPE + MLA_ROPE)
    q_nope = q[..., :MLA_NOPE]
    q_rope = apply_rope(q[..., MLA_NOPE:], cos[:, :, None], sin[:, :, None])
    kv = (rms_norm(c_kv, kv_norm) @ w_ukv).reshape(B, S, MLA_HEADS, MLA_NOPE + MLA_V)
    k_nope, v = kv[..., :MLA_NOPE], kv[..., MLA_NOPE:]
    k_rope = apply_rope(k_rope, cos, sin)
    scale = (MLA_NOPE + MLA_ROPE) ** -0.5
    nb = S // Q_BLOCK
    qn_b = q_nope.reshape(B, nb, Q_BLOCK, MLA_HEADS, MLA_NOPE).transpose(1, 0, 2, 3, 4)
    qr_b = q_rope.reshape(B, nb, Q_BLOCK, MLA_HEADS, MLA_ROPE).transpose(1, 0, 2, 3, 4)
    key_pos = jnp.arange(S)

    def block(args):
        i, qn, qr = args
        s = (jnp.einsum('bqhd,bkhd->bhqk', qn, k_nope)
             + jnp.einsum('bqhr,bkr->bhqk', qr, k_rope)).astype(jnp.float32) * scale
        qpos = i * Q_BLOCK + jnp.arange(Q_BLOCK)
        s = jnp.where(key_pos[None, :] <= qpos[:, None], s, -jnp.inf)
        p = jax.nn.softmax(s, axis=-1).astype(v.dtype)
        return jnp.einsum('bhqk,bkhd->bqhd', p, v)

    o = lax.map(block, (jnp.arange(nb), qn_b, qr_b))
    return o.transpose(1, 0, 2, 3, 4).reshape(B, S, MLA_W)


def gla_chunk_scan(q, k, v, gk):
    B, S, H, DK = q.shape
    DV = v.shape[-1]
    causal = jnp.tril(jnp.ones((CHUNK, CHUNK), dtype=bool))
    xs = tuple(to_chunks(t.astype(jnp.float32)) for t in (q, k, v, gk))

    def step(state, xc):
        qc, kc, vc, gc = xc
        b = jnp.cumsum(gc, axis=-2)
        o_inter = jnp.einsum('bhcd,bhde->bhce', qc * jnp.exp(b), state)
        diff = b[:, :, :, None, :] - b[:, :, None, :, :]
        decay = jnp.exp(jnp.where(causal[:, :, None], diff, -jnp.inf))
        a = jnp.einsum('bhid,bhjd,bhijd->bhij', qc, kc, decay)
        o = o_inter + jnp.einsum('bhij,bhje->bhie', a, vc)
        b_last = b[:, :, -1:, :]
        new_state = (jnp.exp(b_last[:, :, 0, :, None]) * state
                     + jnp.einsum('bhcd,bhce->bhde', kc * jnp.exp(b_last - b), vc))
        return new_state, o

    init = jnp.zeros((B, H, DK, DV), jnp.float32)
    _, o = lax.scan(step, init, xs)
    return from_chunks(o)


def gla_branch(q, k, v, g_lr, r, w_gate2, gate_bias, out_norm):
    B, S, _ = q.shape
    qh = q.reshape(B, S, GLA_HEADS, GLA_DK) * (GLA_DK ** -0.5)
    kh = k.reshape(B, S, GLA_HEADS, GLA_DK)
    vh = v.reshape(B, S, GLA_HEADS, GLA_DV)
    gk = jax.nn.log_sigmoid((g_lr @ w_gate2 + gate_bias).astype(jnp.float32)) / GLA_GATE_NORM
    gk = gk.reshape(B, S, GLA_HEADS, GLA_DK)
    o = gla_chunk_scan(qh, kh, vh, gk)
    o = rms_norm(o, out_norm.reshape(GLA_HEADS, GLA_DV)).astype(v.dtype)
    return o.reshape(B, S, GLA_W) * jax.nn.silu(r)


def gated_delta_chunked(q, k, v, beta, g):
    B, S, H, DK = q.shape
    DV = v.shape[-1]
    qc, kc, vc = (to_chunks(t.astype(jnp.float32)) for t in (q, k, v))
    bc = to_chunks(beta[..., None].astype(jnp.float32))[..., 0]
    G = jnp.cumsum(to_chunks(g[..., None].astype(jnp.float32))[..., 0], axis=-1)
    causal = jnp.tril(jnp.ones((CHUNK, CHUNK), dtype=bool))
    strict = jnp.tril(jnp.ones((CHUNK, CHUNK), dtype=bool), -1)
    gam = jnp.exp(jnp.where(causal, G[..., :, None] - G[..., None, :], -jnp.inf))
    kk = jnp.einsum('nbhid,nbhjd->nbhij', kc, kc)
    t_mat = jnp.where(strict, bc[..., :, None] * gam * kk, 0.0) + jnp.eye(CHUNK, dtype=jnp.float32)
    w = lax.linalg.triangular_solve(t_mat, (bc * jnp.exp(G))[..., None] * kc,
                                    left_side=True, lower=True, unit_diagonal=True)
    u0 = lax.linalg.triangular_solve(t_mat, bc[..., None] * vc,
                                     left_side=True, lower=True, unit_diagonal=True)
    qk = gam * jnp.einsum('nbhid,nbhjd->nbhij', qc, kc)
    q_dec = qc * jnp.exp(G)[..., None]
    k_dec = kc * jnp.exp(G[..., -1:] - G)[..., None]
    g_tot = jnp.exp(G[..., -1])

    def step(state, xc):
        qd, wc, u0c, qkc, kd, gt = xc
        u = u0c - jnp.einsum('bhcd,bhde->bhce', wc, state)
        o = jnp.einsum('bhcd,bhde->bhce', qd, state) + jnp.einsum('bhij,bhje->bhie', qkc, u)
        new_state = gt[..., None, None] * state + jnp.einsum('bhcd,bhce->bhde', kd, u)
        return new_state, o

    init = jnp.zeros((B, H, DK, DV), jnp.float32)
    _, o = lax.scan(step, init, (q_dec, w, u0, qk, k_dec, g_tot))
    return from_chunks(o)


def gdn_branch(qkv, b_logit, a_logit, z, conv_w, a_log, dt_bias, out_norm):
    B, S, _ = qkv.shape
    qkv = jax.nn.silu(causal_dwconv(qkv, conv_w))
    q, k, v = jnp.split(qkv, [GDN_KW, 2 * GDN_KW], axis=-1)
    rep = GDN_V_HEADS // GDN_QK_HEADS
    q = jnp.repeat(l2_normalize(q.reshape(B, S, GDN_QK_HEADS, GDN_DK)), rep, axis=2) * (GDN_DK ** -0.5)
    k = jnp.repeat(l2_normalize(k.reshape(B, S, GDN_QK_HEADS, GDN_DK)), rep, axis=2)
    vh = v.reshape(B, S, GDN_V_HEADS, GDN_DV)
    beta = jax.nn.sigmoid(b_logit.astype(jnp.float32))
    g = -jnp.exp(a_log.astype(jnp.float32)) * jax.nn.softplus(
        a_logit.astype(jnp.float32) + dt_bias.astype(jnp.float32))
    o = gated_delta_chunked(q, k, vh, beta, g)
    o = rms_norm(o, out_norm.reshape(GDN_V_HEADS, GDN_DV)).astype(qkv.dtype)
    return o.reshape(B, S, GDN_W) * jax.nn.silu(z)


def token_mixer(h, cos, sin, w_in, mla_q_norm, mla_kv_norm, mla_w_uq, mla_w_ukv,
                gla_w_gate2, gla_gate_bias, gla_out_norm, gdn_conv_w, gdn_a_log,
                gdn_dt_bias, gdn_out_norm, w_branch_mla, w_branch_gla, w_branch_gdn, w_out):
    (c_q, c_kv, k_rope, gla_q, gla_k, gla_v, gla_g_lr, gla_r,
     gdn_qkv, gdn_b, gdn_a, gdn_z, merge) = jnp.split(h @ w_in, _split_points(), axis=-1)
    y_mla = mla_branch(c_q, c_kv, k_rope, cos, sin, mla_q_norm, mla_kv_norm, mla_w_uq, mla_w_ukv)
    y_gla = gla_branch(gla_q, gla_k, gla_v, gla_g_lr, gla_r, gla_w_gate2, gla_gate_bias, gla_out_norm)
    y_gdn = gdn_branch(gdn_qkv, gdn_b, gdn_a, gdn_z, gdn_conv_w, gdn_a_log, gdn_dt_bias, gdn_out_norm)
    g_mla, g_gla, g_gdn = jnp.split(jax.nn.sigmoid(merge), N_BRANCH, axis=-1)
    mixed = (g_mla * (y_mla @ w_branch_mla) + g_gla * (y_gla @ w_branch_gla)
             + g_gdn * (y_gdn @ w_branch_gdn))
    return mixed @ w_out


def conv_ffn(h, w_up, conv_w, conv_b, w_down):
    gate, up = jnp.split(h @ w_up, 2, axis=-1)
    gate = causal_dwconv(gate, conv_w) + conv_b
    return (jax.nn.silu(gate) * up) @ w_down


def setup_inputs(seed: int = 0) -> dict:
    key = jax.random.key(seed)
    ks = jax.random.split(key, 32)
    L = DEPTH
    f32 = jnp.float32

    def dense(k, fan_in, fan_out):
        return jax.random.normal(k, (L, fan_in, fan_out), f32) * fan_in ** -0.5

    def gain(k, *shape):
        return 1.0 + 0.02 * jax.random.normal(k, shape, f32)

    x = jax.random.normal(ks[0], (BATCH, SEQ, D_MODEL), f32)
    positions = (jax.random.randint(ks[1], (BATCH, 1), 0, 4096, dtype=jnp.int32)
                 + jnp.arange(SEQ, dtype=jnp.int32)[None, :]).astype(jnp.int32)
    dt = jnp.exp(jax.random.uniform(ks[14], (L, GDN_V_HEADS), f32,
                                    minval=math.log(1e-3), maxval=math.log(1e-1)))
    return {
        "x": x,
        "positions": positions,
        "attn_norm": gain(ks[2], L, D_MODEL),
        "w_in": dense(ks[3], D_MODEL, IN_DIM),
        "mla_q_norm": gain(ks[4], L, MLA_Q_RANK),
        "mla_kv_norm": gain(ks[5], L, MLA_KV_RANK),
        "mla_w_uq": dense(ks[6], MLA_Q_RANK, MLA_HEADS * (MLA_NOPE + MLA_ROPE)),
        "mla_w_ukv": dense(ks[7], MLA_KV_RANK, MLA_HEADS * (MLA_NOPE + MLA_V)),
        "gla_w_gate2": dense(ks[8], GLA_GATE_RANK, GLA_KW),
        "gla_gate_bias": 0.02 * jax.random.normal(ks[9], (L, GLA_KW), f32),
        "gla_out_norm": gain(ks[10], L, GLA_W),
        "gdn_conv_w": jax.random.normal(ks[11], (L, CONV_WIDTH, GDN_CONV_DIM), f32) * CONV_WIDTH ** -0.5,
        "gdn_a_log": jnp.log(jax.random.uniform(ks[12], (L, GDN_V_HEADS), f32, minval=1.0, maxval=16.0)),
        "gdn_dt_bias": dt + jnp.log(-jnp.expm1(-dt)),
        "gdn_out_norm": gain(ks[13], L, GDN_W),
        "w_branch_mla": dense(ks[15], MLA_W, D_MODEL),
        "w_branch_gla": dense(ks[16], GLA_W, D_MODEL),
        "w_branch_gdn": dense(ks[17], GDN_W, D_MODEL),
        "w_out": dense(ks[18], D_MODEL, D_MODEL),
        "ffn_norm": gain(ks[19], L, D_MODEL),
        "ffn_w_up": dense(ks[20], D_MODEL, 2 * D_FF),
        "ffn_conv_w": jax.random.normal(ks[21], (L, FFN_CONV, D_FF), f32) * FFN_CONV ** -0.5,
        "ffn_conv_b": 0.01 * jax.random.normal(ks[22], (L, D_FF), f32),
        "ffn_w_down": dense(ks[23], D_FF, D_MODEL),
        "final_norm": gain(ks[24], D_MODEL),
    }


def reference(x, positions, attn_norm, w_in, mla_q_norm, mla_kv_norm, mla_w_uq, mla_w_ukv,
              gla_w_gate2, gla_gate_bias, gla_out_norm, gdn_conv_w, gdn_a_log, gdn_dt_bias,
              gdn_out_norm, w_branch_mla, w_branch_gla, w_branch_gdn, w_out, ffn_norm,
              ffn_w_up, ffn_conv_w, ffn_conv_b, ffn_w_down, final_norm):
    cos, sin = rope_tables(positions)
    for l in range(DEPTH):
        h = rms_norm(x, attn_norm[l])
        x = x + token_mixer(h, cos, sin, w_in[l], mla_q_norm[l], mla_kv_norm[l], mla_w_uq[l],
                            mla_w_ukv[l], gla_w_gate2[l], gla_gate_bias[l], gla_out_norm[l],
                            gdn_conv_w[l], gdn_a_log[l], gdn_dt_bias[l], gdn_out_norm[l],
                            w_branch_mla[l], w_branch_gla[l], w_branch_gdn[l], w_out[l])
        h = rms_norm(x, ffn_norm[l])
        x = x + conv_ffn(h, ffn_w_up[l], ffn_conv_w[l], ffn_conv_b[l], ffn_w_down[l])
    return rms_norm(x, final_norm)
```

```python
import functools

import numpy as np
import jax
import jax.numpy as jnp
from jax import lax
from jax.experimental import pallas as pl
from jax.experimental.pallas import tpu as pltpu

F32 = jnp.float32
BF16 = jnp.bfloat16

EPS = 1e-6
D_MODEL = 2048
MLA_HEADS = 8
MLA_NOPE = 128
MLA_ROPE = 64
MLA_V = 128
MLA_QK = MLA_NOPE + MLA_ROPE
MLA_RANK = 512
ROPE_THETA = 10000.0
GLA_HEADS = 4
GLA_DK = 128
GLA_DV = 256
GLA_GATE_RANK = 16
GLA_GATE_NORM = 16.0
GDN_QK_HEADS = 4
GDN_V_HEADS = 8
GDN_DK = 128
GDN_DV = 128
GDN_CONV = 4
CHUNK = 64
D_FF = 5632
FFN_CONV = 3

BIG_CQ = 0
BIG_CKV = 512
BIG_GLA_Q = 1024
BIG_GLA_K = 1536
BIG_GLA_V = 2048
BIG_GLA_R = 3072
BIG_GDN_QKV = 4096
BIG_GDN_Z = 6144
BIG_MERGE = 7168
BIG_W = 13312
SM_KROPE = 0
SM_KROT = 64
SM_GLR = 128
SM_GDN_B = 144
SM_GDN_A = 152
SM_W = 256

VMEM_LIMIT_BYTES = 56 * 1024 * 1024
NEG_BIG = -1e30


def _params(*semantics):
    return pltpu.CompilerParams(dimension_semantics=semantics, vmem_limit_bytes=VMEM_LIMIT_BYTES)


def _dot(a, b):
    return jnp.dot(a, b, preferred_element_type=F32)


def _dot_nt(a, b):
    return lax.dot_general(a, b, (((1,), (1,)), ((), ())), preferred_element_type=F32)


def _dot_tn(a, b):
    return lax.dot_general(a, b, (((0,), (0,)), ((), ())), preferred_element_type=F32)


def _split2(x):
    hi = x.astype(BF16)
    lo = (x - hi.astype(F32)).astype(BF16)
    return hi, lo


def _split3(x):
    hi = x.astype(BF16)
    r = x - hi.astype(F32)
    mid = r.astype(BF16)
    lo = (r - mid.astype(F32)).astype(BF16)
    return hi, mid, lo


def _dot_hp(a, b):
    ah, al = _split2(a)
    bh, bl = _split2(b)
    return _dot(ah, bh) + (_dot(ah, bl) + _dot(al, bh))


def _dot_exact_lhs(m_bf16, x):
    h, m, l = _split3(x)
    return _dot(m_bf16, h) + (_dot(m_bf16, m) + _dot(m_bf16, l))


def _rms(xf, g):
    return xf * lax.rsqrt(jnp.mean(xf * xf, axis=-1, keepdims=True) + EPS) * g


def _sigmoid(x):
    return 1.0 / (1.0 + jnp.exp(-x))


def _silu(x):
    return x * _sigmoid(x)


def _softplus(x):
    return jnp.maximum(x, 0.0) + jnp.log(1.0 + jnp.exp(-jnp.abs(x)))


def _rms_matmul_kernel(x_ref, g_ref, w_ref, o_ref, h_ref):
    @pl.when(pl.program_id(1) == 0)
    def _():
        h_ref[...] = _rms(x_ref[...].astype(F32), g_ref[...]).astype(BF16)

    o_ref[...] = _dot(h_ref[...], w_ref[...]).astype(o_ref.dtype)


def rms_matmul(x, g, w, *, out_dtype, tm, tn):
    s = x.shape[0]
    k, n = w.shape
    return pl.pallas_call(
        _rms_matmul_kernel,
        grid=(s // tm, n // tn),
        in_specs=[
            pl.BlockSpec((tm, k), lambda i, j: (i, 0)),
            pl.BlockSpec((1, k), lambda i, j: (0, 0)),
            pl.BlockSpec((k, tn), lambda i, j: (0, j)),
        ],
        out_specs=pl.BlockSpec((tm, tn), lambda i, j: (i, j)),
        out_shape=jax.ShapeDtypeStruct((s, n), out_dtype),
        scratch_shapes=[pltpu.VMEM((tm, k), BF16)],
        compiler_params=_params("parallel", "arbitrary"),
        name="rms_matmul",
    )(x, g.reshape(1, k), w)


def _matmul_residual_kernel(a_ref, w_ref, r_ref, o_ref):
    @pl.when(pl.program_id(1) == 0)
    def _():
        o_ref[...] = r_ref[...]

    o_ref[...] += _dot(a_ref[...], w_ref[...])


def matmul_residual(a, w, res, *, tm, tk):
    s, k = a.shape
    n = w.shape[1]
    return pl.pallas_call(
        _matmul_residual_kernel,
        grid=(s // tm, k // tk),
        in_specs=[
            pl.BlockSpec((tm, tk), lambda i, kk: (i, kk)),
            pl.BlockSpec((tk, n), lambda i, kk: (kk, 0)),
            pl.BlockSpec((tm, n), lambda i, kk: (i, 0)),
        ],
        out_specs=pl.BlockSpec((tm, n), lambda i, kk: (i, 0)),
        out_shape=jax.ShapeDtypeStruct((s, n), F32),
        compiler_params=_params("parallel", "arbitrary"),
        name="matmul_residual",
    )(a, w, res)


def _mla_prep_kernel(cq_ref, ckv_ref, sm_ref, cos_ref, sin_ref, qn_ref, kvn_ref, wq_ref, wkv_ref,
                     q_out, k_out, v_out):
    scale = MLA_QK ** -0.5
    hq = _rms(cq_ref[...].astype(F32), qn_ref[...]).astype(BF16)
    qa = _dot(hq, wq_ref[...])
    cos = cos_ref[...]
    sin = sin_ref[...]
    nope_w = MLA_HEADS * MLA_NOPE
    rope_w = MLA_HEADS * MLA_ROPE
    qr = qa[:, nope_w:nope_w + rope_w] * cos + qa[:, nope_w + rope_w:] * sin
    for h in range(MLA_HEADS):
        q_out[h, :, 0:MLA_NOPE] = (qa[:, h * MLA_NOPE:(h + 1) * MLA_NOPE] * scale).astype(BF16)
        q_out[h, :, MLA_NOPE:MLA_QK] = (qr[:, h * MLA_ROPE:(h + 1) * MLA_ROPE] * scale).astype(BF16)

    hkv = _rms(ckv_ref[...].astype(F32), kvn_ref[...]).astype(BF16)
    kva = _dot(hkv, wkv_ref[...])
    sm = sm_ref[...]
    kr = (sm[:, SM_KROPE:SM_KROPE + MLA_ROPE] * cos[:, 0:MLA_ROPE]
          + sm[:, SM_KROT:SM_KROT + MLA_ROPE] * sin[:, 0:MLA_ROPE]).astype(BF16)
    for h in range(MLA_HEADS):
        k_out[h, :, 0:MLA_NOPE] = kva[:, h * MLA_NOPE:(h + 1) * MLA_NOPE].astype(BF16)
        k_out[h, :, MLA_NOPE:MLA_QK] = kr
    v_out[...] = kva[:, nope_w:].astype(BF16)


def mla_prep(big, small, cos8, sin8, q_norm, kv_norm, wq, wkv, *, tm):
    s = big.shape[0]
    rope_w = MLA_HEADS * MLA_ROPE
    return pl.pallas_call(
        _mla_prep_kernel,
        grid=(s // tm,),
        in_specs=[
            pl.BlockSpec((tm, MLA_RANK), lambda i: (i, BIG_CQ // MLA_RANK)),
            pl.BlockSpec((tm, MLA_RANK), lambda i: (i, BIG_CKV // MLA_RANK)),
            pl.BlockSpec((tm, SM_W), lambda i: (i, 0)),
            pl.BlockSpec((tm, rope_w), lambda i: (i, 0)),
            pl.BlockSpec((tm, rope_w), lambda i: (i, 0)),
            pl.BlockSpec((1, MLA_RANK), lambda i: (0, 0)),
            pl.BlockSpec((1, MLA_RANK), lambda i: (0, 0)),
            pl.BlockSpec(wq.shape, lambda i: (0, 0)),
            pl.BlockSpec(wkv.shape, lambda i: (0, 0)),
        ],
        out_specs=[
            pl.BlockSpec((MLA_HEADS, tm, MLA_QK), lambda i: (0, i, 0)),
            pl.BlockSpec((MLA_HEADS, tm, MLA_QK), lambda i: (0, i, 0)),
            pl.BlockSpec((tm, MLA_HEADS * MLA_V), lambda i: (i, 0)),
        ],
        out_shape=[
            jax.ShapeDtypeStruct((MLA_HEADS, s, MLA_QK), BF16),
            jax.ShapeDtypeStruct((MLA_HEADS, s, MLA_QK), BF16),
            jax.ShapeDtypeStruct((s, MLA_HEADS * MLA_V), BF16),
        ],
        compiler_params=_params("parallel"),
        name="mla_prep",
    )(big, big, small, cos8, sin8, q_norm.reshape(1, -1), kv_norm.reshape(1, -1), wq, wkv)


def _flash_kernel(qi_ref, ki_ref, q_ref, k_ref, v_ref, o_ref, m_ref, l_ref, acc_ref, *, tq, tk):
    p = pl.program_id(1)
    qi = qi_ref[p]
    ki = ki_ref[p]

    @pl.when(ki == 0)
    def _():
        m_ref[...] = jnp.full(m_ref.shape, NEG_BIG, F32)
        l_ref[...] = jnp.zeros(l_ref.shape, F32)
        acc_ref[...] = jnp.zeros(acc_ref.shape, F32)

    def step(masked):
        s = _dot_nt(q_ref[0], k_ref[0])
        if masked:
            row = qi * tq + lax.broadcasted_iota(jnp.int32, (tq, tk), 0)
            col = ki * tk + lax.broadcasted_iota(jnp.int32, (tq, tk), 1)
            s = jnp.where(col <= row, s, NEG_BIG)
        m_prev = m_ref[...]
        m_new = jnp.maximum(m_prev, jnp.max(s, axis=-1, keepdims=True))
        alpha = jnp.exp(m_prev - m_new)
        pr = jnp.exp(s - m_new)
        l_ref[...] = alpha * l_ref[...] + jnp.sum(pr, axis=-1, keepdims=True)
        acc_ref[...] = alpha * acc_ref[...] + _dot(pr.astype(BF16), v_ref[...])
        m_ref[...] = m_new

    crosses = (ki + 1) * tk - 1 > qi * tq

    @pl.when(crosses)
    def _():
        step(True)

    @pl.when(jnp.logical_not(crosses))
    def _():
        step(False)

    @pl.when(ki == ((qi + 1) * tq - 1) // tk)
    def _():
        o_ref[...] = (acc_ref[...] / l_ref[...]).astype(o_ref.dtype)


def flash_attention(qc, kc, v, *, tq, tk):
    s = qc.shape[1]
    nq = s // tq
    qi_list, ki_list = [], []
    for qi in range(nq):
        for ki in range(((qi + 1) * tq - 1) // tk + 1):
            qi_list.append(qi)
            ki_list.append(ki)
    qi_arr = jnp.asarray(np.array(qi_list, np.int32))
    ki_arr = jnp.asarray(np.array(ki_list, np.int32))
    grid_spec = pltpu.PrefetchScalarGridSpec(
        num_scalar_prefetch=2,
        grid=(MLA_HEADS, len(qi_list)),
        in_specs=[
            pl.BlockSpec((1, tq, MLA_QK), lambda h, p, qi, ki: (h, qi[p], 0)),
            pl.BlockSpec((1, tk, MLA_QK), lambda h, p, qi, ki: (h, ki[p], 0)),
            pl.BlockSpec((tk, MLA_V), lambda h, p, qi, ki: (ki[p], h)),
        ],
        out_specs=pl.BlockSpec((tq, MLA_V), lambda h, p, qi, ki: (qi[p], h)),
        scratch_shapes=[
            pltpu.VMEM((tq, 1), F32),
            pltpu.VMEM((tq, 1), F32),
            pltpu.VMEM((tq, MLA_V), F32),
        ],
    )
    return pl.pallas_call(
        functools.partial(_flash_kernel, tq=tq, tk=tk),
        grid_spec=grid_spec,
        out_shape=jax.ShapeDtypeStruct((s, MLA_HEADS * MLA_V), BF16),
        compiler_params=_params("parallel", "arbitrary"),
        name="flash_attention",
    )(qi_arr, ki_arr, qc, kc, v)


_GLA_LEVELS = 6
_GLA_E_BINCL = 0
_GLA_E_KSTATE = 1
_GLA_E_Q0 = 2
_GLA_E_K0 = 2 + _GLA_LEVELS
_GLA_E_BLAST = 2 + 2 * _GLA_LEVELS
_GLA_E_ROWS = CHUNK * _GLA_E_BLAST + 8


def _gla_constants():
    c = CHUNK
    i = np.arange(c)[:, None]
    t = np.arange(c)[None, :]
    sel = np.zeros((_GLA_E_ROWS, c), np.float32)
    sel[_GLA_E_BINCL * c:(_GLA_E_BINCL + 1) * c] = t <= i
    sel[_GLA_E_KSTATE * c:(_GLA_E_KSTATE + 1) * c] = t > i
    masks = np.zeros((_GLA_LEVELS + 1, c, c), np.float32)
    for lvl in range(_GLA_LEVELS):
        half = c >> (lvl + 1)
        blk = 2 * half
        pos = i % blk
        mid = (i // blk) * blk + half
        sel[(_GLA_E_Q0 + lvl) * c:(_GLA_E_Q0 + lvl + 1) * c] = (pos >= half) & (t >= mid) & (t <= i)
        sel[(_GLA_E_K0 + lvl) * c:(_GLA_E_K0 + lvl + 1) * c] = (pos < half) & (t > i) & (t <= mid - 1)
        j = np.arange(c)[None, :]
        masks[lvl] = (i // blk == j // blk) & (pos >= half) & (j % blk < half)
    sel[_GLA_E_BLAST * c:] = 1.0
    masks[_GLA_LEVELS] = np.eye(c)
    return sel, masks


def _gla_kernel(q_ref, k_ref, v_ref, r_ref, sm_ref, w2_ref, bias_ref, sel_ref, mask_ref, gn_ref,
                y_ref, st_ref):
    c = CHUNK

    @pl.when(pl.program_id(0) == 0)
    def _():
        st_ref[...] = jnp.zeros(st_ref.shape, F32)

    z = _dot_hp(sm_ref[:, SM_GLR:], w2_ref[...]) + bias_ref[...]
    gk = (jnp.minimum(z, 0.0) - jnp.log(1.0 + jnp.exp(-jnp.abs(z)))) * (1.0 / GLA_GATE_NORM)
    gh, gl = _split2(gk)
    sel = sel_ref[...]
    e_all = jnp.exp(_dot(sel, gh) + _dot(sel, gl))

    def e_blk(idx, cols):
        return e_all[idx * c:(idx + 1) * c, cols]

    for h in range(GLA_HEADS):
        kc = slice(h * GLA_DK, (h + 1) * GLA_DK)
        vc = slice(h * GLA_DV, (h + 1) * GLA_DV)
        qs = q_ref[:, kc].astype(F32) * (GLA_DK ** -0.5)
        kb = k_ref[:, kc]
        kf = kb.astype(F32)
        vb = v_ref[:, vc]
        a = mask_ref[_GLA_LEVELS] * _dot_nt(qs.astype(BF16), kb)
        for lvl in range(_GLA_LEVELS):
            ql = (qs * e_blk(_GLA_E_Q0 + lvl, kc)).astype(BF16)
            kl = (kf * e_blk(_GLA_E_K0 + lvl, kc)).astype(BF16)
            a = a + mask_ref[lvl] * _dot_nt(ql, kl)
        st = st_ref[h]
        qb = (qs * e_blk(_GLA_E_BINCL, kc)).astype(BF16)
        o = _dot_nt(qb, st.astype(BF16)) + _dot(a.astype(BF16), vb)
        kd = (kf * e_blk(_GLA_E_KSTATE, kc)).astype(BF16)
        e_last = e_all[_GLA_E_BLAST * c:_GLA_E_BLAST * c + 1, kc]
        st_ref[h] = st * e_last + _dot_tn(vb, kd)
        o = _rms(o, gn_ref[:, vc])
        y_ref[:, vc] = (o * _silu(r_ref[:, vc].astype(F32))).astype(y_ref.dtype)


def gla_chunk(big, small, w_gate2, gate_bias, out_norm):
    s = big.shape[0]
    c = CHUNK
    sel, masks = _gla_constants()
    kw = GLA_HEADS * GLA_DK
    vw = GLA_HEADS * GLA_DV
    w2_pad = jnp.zeros((SM_W - SM_GLR, kw), F32).at[0:GLA_GATE_RANK].set(w_gate2)
    return pl.pallas_call(
        _gla_kernel,
        grid=(s // c,),
        in_specs=[
            pl.BlockSpec((c, kw), lambda i: (i, BIG_GLA_Q // kw)),
            pl.BlockSpec((c, kw), lambda i: (i, BIG_GLA_K // kw)),
            pl.BlockSpec((c, vw), lambda i: (i, BIG_GLA_V // vw)),
            pl.BlockSpec((c, vw), lambda i: (i, BIG_GLA_R // vw)),
            pl.BlockSpec((c, SM_W), lambda i: (i, 0)),
            pl.BlockSpec((SM_W - SM_GLR, kw), lambda i: (0, 0)),
            pl.BlockSpec((1, kw), lambda i: (0, 0)),
            pl.BlockSpec(sel.shape, lambda i: (0, 0)),
            pl.BlockSpec(masks.shape, lambda i: (0, 0, 0)),
            pl.BlockSpec((1, vw), lambda i: (0, 0)),
        ],
        out_specs=pl.BlockSpec((c, vw), lambda i: (i, 0)),
        out_shape=jax.ShapeDtypeStruct((s, vw), BF16),
        scratch_shapes=[pltpu.VMEM((GLA_HEADS, GLA_DV, GLA_DK), F32)],
        compiler_params=_params("arbitrary"),
        name="gla_chunk",
    )(big, big, big, big, small, w2_pad, gate_bias.reshape(1, kw),
      jnp.asarray(sel, BF16), jnp.asarray(masks), out_norm.reshape(1, vw))


_GDN_LEVELS = 6
_GDN_HALO = 16


def _gdn_constants():
    c = CHUNK
    i = np.arange(c)[:, None]
    j = np.arange(c)[None, :]
    cum = np.concatenate([(j <= i), (j > i)], axis=0).astype(np.float32)
    lvl_masks = np.zeros((_GDN_LEVELS, c, c), np.float32)
    for lvl in range(_GDN_LEVELS):
        half = 1 << lvl
        blk = 2 * half
        lvl_masks[lvl] = (i // blk == j // blk) & (i % blk >= half) & (j % blk < half)
    tri = np.stack([(i >= j), (i > j), (i == j)]).astype(np.float32)
    return cum, lvl_masks, tri


def _gdn_kernel(x_ref, halo_ref, z_ref, sm_ref, cw_ref, alog_ref, dtb_ref, cum_ref,
                lvl_ref, tri_ref, gn_ref, y_ref, st_ref, xs_ref):
    c = CHUNK
    kw = GDN_QK_HEADS * GDN_DK
    step = pl.program_id(0)

    @pl.when(step == 0)
    def _():
        st_ref[...] = jnp.zeros(st_ref.shape, F32)

    halo = halo_ref[...].astype(F32)
    xs_ref[0:_GDN_HALO, :] = jnp.where(step > 0, halo, 0.0)
    xs_ref[_GDN_HALO:, :] = x_ref[...].astype(F32)
    conv = cw_ref[0:1, :] * xs_ref[pl.ds(_GDN_HALO - 3, c), :]
    for tap in range(1, GDN_CONV):
        conv = conv + cw_ref[tap:tap + 1, :] * xs_ref[pl.ds(_GDN_HALO - 3 + tap, c), :]
    qkv = _silu(conv)

    slab = sm_ref[:, SM_GLR:]
    b0 = SM_GDN_B - SM_GLR
    a0 = SM_GDN_A - SM_GLR
    beta = _sigmoid(slab)
    g = -jnp.exp(alog_ref[...]) * _softplus(slab + dtb_ref[...])
    cums = _dot_exact_lhs(cum_ref[...], g)
    e_g = jnp.exp(cums[0:c])
    e_gk = jnp.exp(cums[c:2 * c])
    e_last = e_g[c - 1:c, :]
    low = cum_ref[0:c, :]

    causal = tri_ref[0]
    strict = tri_ref[1]
    eye = tri_ref[2]

    kk, qk0, qn, kn = [], [], [], []
    for hq in range(GDN_QK_HEADS):
        qr = qkv[:, hq * GDN_DK:(hq + 1) * GDN_DK]
        kr = qkv[:, kw + hq * GDN_DK:kw + (hq + 1) * GDN_DK]
        qh = qr * lax.rsqrt(jnp.sum(qr * qr, axis=-1, keepdims=True) + EPS) * (GDN_DK ** -0.5)
        kh = kr * lax.rsqrt(jnp.sum(kr * kr, axis=-1, keepdims=True) + EPS)
        kb = kh.astype(BF16)
        qn.append(qh)
        kn.append(kh)
        kk.append(_dot_nt(kb, kb))
        qk0.append(_dot_nt(qh.astype(BF16), kb))

    rep = GDN_V_HEADS // GDN_QK_HEADS
    for hv in range(GDN_V_HEADS):
        hq = hv // rep
        vc = slice(hv * GDN_DV, (hv + 1) * GDN_DV)
        v = qkv[:, 2 * kw + hv * GDN_DV:2 * kw + (hv + 1) * GDN_DV]
        bcol = beta[:, b0 + hv:b0 + hv + 1]
        d = _dot_exact_lhs(low, g[:, a0 + hv:a0 + hv + 1] * strict)
        gam = causal * jnp.exp(d)
        nm = strict * (bcol * gam * kk[hq])
        x = eye - nm * lvl_ref[0]
        for lvl in range(1, _GDN_LEVELS):
            x = x - _dot_hp(x, _dot_hp(nm * lvl_ref[lvl], x))
        egc = e_g[:, a0 + hv:a0 + hv + 1]
        rhs = jnp.concatenate([(bcol * egc) * kn[hq], bcol * v], axis=1)
        wu = _dot_hp(x, rhs)
        w = wu[:, 0:GDN_DK]
        u0 = wu[:, GDN_DK:]
        st = st_ref[hv]
        stb = st.astype(BF16)
        u = u0 - _dot(w.astype(BF16), stb)
        ub = u.astype(BF16)
        o = _dot((qn[hq] * egc).astype(BF16), stb) + _dot((gam * qk0[hq]).astype(BF16), ub)
        kd = (kn[hq] * e_gk[:, a0 + hv:a0 + hv + 1]).astype(BF16)
        st_ref[hv] = st * e_last[:, a0 + hv:a0 + hv + 1] + _dot_tn(kd, ub)
        o = _rms(o, gn_ref[:, vc])
        y_ref[:, vc] = (o * _silu(z_ref[:, vc].astype(F32))).astype(y_ref.dtype)


def gdn_chunk(big, small, conv_w, a_log, dt_bias, out_norm):
    s = big.shape[0]
    c = CHUNK
    cw = 2 * GDN_QK_HEADS * GDN_DK + GDN_V_HEADS * GDN_DV
    vw = GDN_V_HEADS * GDN_DV
    cum, lvl_masks, tri = _gdn_constants()
    halo_per_chunk = c // _GDN_HALO
    slab_w = SM_W - SM_GLR
    a0 = SM_GDN_A - SM_GLR
    alog_slab = jnp.zeros((1, slab_w), F32).at[0, a0:a0 + GDN_V_HEADS].set(a_log)
    dtb_slab = jnp.zeros((1, slab_w), F32).at[0, a0:a0 + GDN_V_HEADS].set(dt_bias)
    const2 = lambda i: (0, 0)
    const3 = lambda i: (0, 0, 0)
    return pl.pallas_call(
        _gdn_kernel,
        grid=(s // c,),
        in_specs=[
            pl.BlockSpec((c, cw), lambda i: (i, BIG_GDN_QKV // cw)),
            pl.BlockSpec((_GDN_HALO, cw),
                         lambda i: (jnp.maximum(i * halo_per_chunk - 1, 0), BIG_GDN_QKV // cw)),
            pl.BlockSpec((c, vw), lambda i: (i, BIG_GDN_Z // vw)),
            pl.BlockSpec((c, SM_W), lambda i: (i, 0)),
            pl.BlockSpec((GDN_CONV, cw), const2),
            pl.BlockSpec((1, slab_w), const2),
            pl.BlockSpec((1, slab_w), const2),
            pl.BlockSpec(cum.shape, const2),
            pl.BlockSpec(lvl_masks.shape, const3),
            pl.BlockSpec(tri.shape, const3),
            pl.BlockSpec((1, vw), const2),
        ],
        out_specs=pl.BlockSpec((c, vw), lambda i: (i, 0)),
        out_shape=jax.ShapeDtypeStruct((s, vw), BF16),
        scratch_shapes=[
            pltpu.VMEM((GDN_V_HEADS, GDN_DK, GDN_DV), F32),
            pltpu.VMEM((_GDN_HALO + c, cw), F32),
        ],
        compiler_params=_params("arbitrary"),
        name="gdn_chunk",
    )(big, big, big, small, conv_w, alog_slab, dtb_slab,
      jnp.asarray(cum, BF16), jnp.asarray(lvl_masks), jnp.asarray(tri), out_norm.reshape(1, vw))


def _gated_mix_kernel(y0_ref, y1_ref, y2_ref, w0_ref, w1_ref, w2_ref, g0_ref, g1_ref, g2_ref, o_ref):
    acc = _sigmoid(g0_ref[...].astype(F32)) * _dot(y0_ref[...], w0_ref[...])
    acc = acc + _sigmoid(g1_ref[...].astype(F32)) * _dot(y1_ref[...], w1_ref[...])
    acc = acc + _sigmoid(g2_ref[...].astype(F32)) * _dot(y2_ref[...], w2_ref[...])
    o_ref[...] = acc.astype(o_ref.dtype)


def gated_mix(ys, ws, big, *, tm, tn):
    s, k = ys[0].shape
    n = ws[0].shape[1]
    y_spec = pl.BlockSpec((tm, k), lambda i, j: (i, 0))
    w_spec = pl.BlockSpec((k, tn), lambda i, j: (0, j))

    def gate_spec(b):
        base = (BIG_MERGE + b * n) // tn
        return pl.BlockSpec((tm, tn), lambda i, j: (i, base + j))

    return pl.pallas_call(
        _gated_mix_kernel,
        grid=(s // tm, n // tn),
        in_specs=[y_spec] * 3 + [w_spec] * 3 + [gate_spec(0), gate_spec(1), gate_spec(2)],
        out_specs=pl.BlockSpec((tm, tn), lambda i, j: (i, j)),
        out_shape=jax.ShapeDtypeStruct((s, n), BF16),
        compiler_params=_params("parallel", "parallel"),
        name="gated_mix",
    )(*ys, *ws, big, big, big)


_FFN_HALO = 8


def _ffn_up_kernel(x_ref, xh_ref, g_ref, wg_ref, wu_ref, cw_ref, cb_ref, o_ref, h_ref, hh_ref, gs_ref):
    tm = x_ref.shape[0]

    @pl.when(pl.program_id(1) == 0)
    def _():
        h_ref[...] = _rms(x_ref[...], g_ref[...]).astype(BF16)
        hh_ref[...] = _rms(xh_ref[...], g_ref[...]).astype(BF16)

    gate_halo = _dot(hh_ref[...], wg_ref[...])
    gs_ref[0:_FFN_HALO, :] = jnp.where(pl.program_id(0) > 0, gate_halo, 0.0)
    gs_ref[_FFN_HALO:, :] = _dot(h_ref[...], wg_ref[...])
    conv = cb_ref[...] + cw_ref[0:1, :] * gs_ref[pl.ds(_FFN_HALO - 2, tm), :]
    for tap in range(1, FFN_CONV):
        conv = conv + cw_ref[tap:tap + 1, :] * gs_ref[pl.ds(_FFN_HALO - 2 + tap, tm), :]
    up = _dot(h_ref[...], wu_ref[...])
    o_ref[...] = (_silu(conv) * up).astype(o_ref.dtype)


def ffn_up(x, g, w_up, conv_w, conv_b, *, tm, tn):
    s, k = x.shape
    halo_per_tile = tm // _FFN_HALO
    n_tiles = D_FF // tn
    return pl.pallas_call(
        _ffn_up_kernel,
        grid=(s // tm, n_tiles),
        in_specs=[
            pl.BlockSpec((tm, k), lambda i, j: (i, 0)),
            pl.BlockSpec((_FFN_HALO, k), lambda i, j: (jnp.maximum(i * halo_per_tile - 1, 0), 0)),
            pl.BlockSpec((1, k), lambda i, j: (0, 0)),
            pl.BlockSpec((k, tn), lambda i, j: (0, j)),
            pl.BlockSpec((k, tn), lambda i, j: (0, n_tiles + j)),
            pl.BlockSpec((FFN_CONV, tn), lambda i, j: (0, j)),
            pl.BlockSpec((1, tn), lambda i, j: (0, j)),
        ],
        out_specs=pl.BlockSpec((tm, tn), lambda i, j: (i, j)),
        out_shape=jax.ShapeDtypeStruct((s, D_FF), BF16),
        scratch_shapes=[
            pltpu.VMEM((tm, k), BF16),
            pltpu.VMEM((_FFN_HALO, k), BF16),
            pltpu.VMEM((_FFN_HALO + tm, tn), F32),
        ],
        compiler_params=_params("parallel", "arbitrary"),
        name="ffn_up",
    )(x, x, g.reshape(1, k), w_up, w_up, conv_w, conv_b.reshape(1, -1))


def _rms_kernel(x_ref, g_ref, o_ref):
    o_ref[...] = _rms(x_ref[...], g_ref[...])


def rms_only(x, g, *, tm):
    s, k = x.shape
    return pl.pallas_call(
        _rms_kernel,
        grid=(s // tm,),
        in_specs=[pl.BlockSpec((tm, k), lambda i: (i, 0)), pl.BlockSpec((1, k), lambda i: (0, 0))],
        out_specs=pl.BlockSpec((tm, k), lambda i: (i, 0)),
        out_shape=jax.ShapeDtypeStruct((s, k), F32),
        compiler_params=_params("parallel"),
        name="final_rms",
    )(x, g.reshape(1, k))


def _in_proj_columns():
    widths = (MLA_RANK, MLA_RANK, MLA_ROPE, 512, 512, 1024, GLA_GATE_RANK, 1024, 2048,
              GDN_V_HEADS, GDN_V_HEADS, 1024, 3 * D_MODEL)
    names = ("c_q", "c_kv", "k_rope", "gla_q", "gla_k", "gla_v", "gla_g", "gla_r", "gdn_qkv",
             "gdn_b", "gdn_a", "gdn_z", "merge")
    off, cols = 0, {}
    for nme, wd in zip(names, widths):
        cols[nme] = np.arange(off, off + wd)
        off += wd
    big = np.concatenate([cols[n] for n in
                          ("c_q", "c_kv", "gla_q", "gla_k", "gla_v", "gla_r", "gdn_qkv", "gdn_z", "merge")])
    return cols, big


def _prep_in_proj(w):
    cols, big = _in_proj_columns()
    kr = w[:, cols["k_rope"]]
    half = MLA_ROPE // 2
    kr_rot = jnp.concatenate([-kr[:, half:], kr[:, :half]], axis=1)
    small = jnp.concatenate(
        [kr, kr_rot, w[:, cols["gla_g"]], w[:, cols["gdn_b"]], w[:, cols["gdn_a"]],
         jnp.zeros((w.shape[0], SM_W - SM_GDN_A - GDN_V_HEADS), w.dtype)], axis=1)
    return w[:, big].astype(BF16), small.astype(BF16)


def _prep_mla_weights(w_uq, w_ukv):
    half = MLA_ROPE // 2
    wq = w_uq.reshape(MLA_RANK, MLA_HEADS, MLA_QK)
    nope = wq[:, :, :MLA_NOPE].reshape(MLA_RANK, -1)
    rope = wq[:, :, MLA_NOPE:]
    rot = jnp.concatenate([-rope[:, :, half:], rope[:, :, :half]], axis=2)
    wq_ext = jnp.concatenate([nope, rope.reshape(MLA_RANK, -1), rot.reshape(MLA_RANK, -1)], axis=1)
    wkv = w_ukv.reshape(MLA_RANK, MLA_HEADS, MLA_NOPE + MLA_V)
    wkv_ext = jnp.concatenate([wkv[:, :, :MLA_NOPE].reshape(MLA_RANK, -1),
                               wkv[:, :, MLA_NOPE:].reshape(MLA_RANK, -1)], axis=1)
    return wq_ext.astype(BF16), wkv_ext.astype(BF16)


def _rope_tables(positions):
    inv = ROPE_THETA ** (-jnp.arange(0, MLA_ROPE, 2, dtype=F32) / MLA_ROPE)
    ang = positions.astype(F32)[0][:, None] * inv
    cos = jnp.tile(jnp.cos(ang), (1, 2 * MLA_HEADS))
    sin = jnp.tile(jnp.sin(ang), (1, 2 * MLA_HEADS))
    return cos, sin


def _tile(s, pref):
    t = pref
    while s % t:
        t //= 2
    return t


def kernel(x, positions, attn_norm, w_in, mla_q_norm, mla_kv_norm, mla_w_uq, mla_w_ukv, gla_w_gate2,
           gla_gate_bias, gla_out_norm, gdn_conv_w, gdn_a_log, gdn_dt_bias, gdn_out_norm, w_branch_mla,
           w_branch_gla, w_branch_gdn, w_out, ffn_norm, ffn_w_up, ffn_conv_w, ffn_conv_b, ffn_w_down,
           final_norm):
    b, s, d = x.shape
    assert b == 1 and d == D_MODEL and s % CHUNK == 0
    depth = w_in.shape[0]
    xs = x.reshape(s, d)
    cos8, sin8 = _rope_tables(positions)
    tm = _tile(s, 1024)
    tm_small = _tile(s, 512)
    tq = _tile(s, 1024)

    for l in range(depth):
        w_big, w_small = _prep_in_proj(w_in[l])
        big = rms_matmul(xs, attn_norm[l], w_big, out_dtype=BF16, tm=tm, tn=1024)
        small = rms_matmul(xs, attn_norm[l], w_small, out_dtype=F32, tm=tm, tn=SM_W)

        wq, wkv = _prep_mla_weights(mla_w_uq[l], mla_w_ukv[l])
        qc, kc, v = mla_prep(big, small, cos8, sin8, mla_q_norm[l], mla_kv_norm[l], wq, wkv, tm=tm_small)
        y_mla = flash_attention(qc, kc, v, tq=tq, tk=tq)
        y_gla = gla_chunk(big, small, gla_w_gate2[l], gla_gate_bias[l], gla_out_norm[l])
        y_gdn = gdn_chunk(big, small, gdn_conv_w[l], gdn_a_log[l], gdn_dt_bias[l], gdn_out_norm[l])

        mixed = gated_mix((y_mla, y_gla, y_gdn),
                          (w_branch_mla[l].astype(BF16), w_branch_gla[l].astype(BF16),
                           w_branch_gdn[l].astype(BF16)), big, tm=tm, tn=512)
        xs = matmul_residual(mixed, w_out[l].astype(BF16), xs, tm=tm, tk=512)

        act = ffn_up(xs, ffn_norm[l], ffn_w_up[l].astype(BF16), ffn_conv_w[l], ffn_conv_b[l],
                     tm=tm, tn=512)
        xs = matmul_residual(act, ffn_w_down[l].astype(BF16), xs, tm=tm, tk=512)

    out = rms_only(xs, final_norm, tm=tm_small)
    return out.reshape(b, s, d)
```

```python
import functools

import numpy as np
import jax
import jax.numpy as jnp
from jax import lax
from jax.experimental import pallas as pl
from jax.experimental.pallas import tpu as pltpu

F32 = jnp.float32
BF16 = jnp.bfloat16

EPS = 1e-6
D_MODEL = 2048
MLA_HEADS = 8
MLA_NOPE = 128
MLA_ROPE = 64
MLA_V = 128
MLA_QK = MLA_NOPE + MLA_ROPE
MLA_RANK = 512
ROPE_THETA = 10000.0
GLA_HEADS = 4
GLA_DK = 128
GLA_DV = 256
GLA_GATE_RANK = 16
GLA_GATE_NORM = 16.0
GDN_QK_HEADS = 4
GDN_V_HEADS = 8
GDN_DK = 128
GDN_DV = 128
GDN_CONV = 4
CHUNK = 64
D_FF = 5632
FFN_CONV = 3

BIG_CQ = 0
BIG_CKV = 512
BIG_GLA_Q = 1024
BIG_GLA_K = 1536
BIG_GLA_V = 2048
BIG_GLA_R = 3072
BIG_GDN_QKV = 4096
BIG_GDN_Z = 6144
BIG_MERGE = 7168
BIG_W = 13312
SM_KROPE = 0
SM_KROT = 64
SM_GLR = 128
SM_GDN_B = 144
SM_GDN_A = 152
SM_W = 256

VMEM_LIMIT_BYTES = 56 * 1024 * 1024
NEG_BIG = -1e30
LOG2_E = 1.4426950408889634


def _params(*semantics):
    return pltpu.CompilerParams(dimension_semantics=semantics, vmem_limit_bytes=VMEM_LIMIT_BYTES)


def _dot(a, b):
    return jnp.dot(a, b, preferred_element_type=F32)


def _dot_nt(a, b):
    return lax.dot_general(a, b, (((1,), (1,)), ((), ())), preferred_element_type=F32)


def _dot_tn(a, b):
    return lax.dot_general(a, b, (((0,), (0,)), ((), ())), preferred_element_type=F32)


def _split2(x):
    hi = x.astype(BF16)
    lo = (x - hi.astype(F32)).astype(BF16)
    return hi, lo


def _split3(x):
    hi = x.astype(BF16)
    r = x - hi.astype(F32)
    mid = r.astype(BF16)
    lo = (r - mid.astype(F32)).astype(BF16)
    return hi, mid, lo


def _dot_hp(a, b):
    ah, al = _split2(a)
    bh, bl = _split2(b)
    return _dot(ah, bh) + (_dot(ah, bl) + _dot(al, bh))


def _dot_exact_lhs(m_bf16, x):
    h, m, l = _split3(x)
    return _dot(m_bf16, h) + (_dot(m_bf16, m) + _dot(m_bf16, l))


def _rms(xf, g):
    return xf * lax.rsqrt(jnp.mean(xf * xf, axis=-1, keepdims=True) + EPS) * g


def _sigmoid(x):
    return 1.0 / (1.0 + jnp.exp(-x))


def _silu(x):
    return x * _sigmoid(x)


def _softplus(x):
    return jnp.maximum(x, 0.0) + jnp.log(1.0 + jnp.exp(-jnp.abs(x)))


def _rms_matmul_kernel(x_ref, g_ref, w_ref, o_ref, h_ref):
    @pl.when(pl.program_id(1) == 0)
    def _():
        h_ref[...] = _rms(x_ref[...].astype(F32), g_ref[...]).astype(BF16)

    o_ref[...] = _dot(h_ref[...], w_ref[...]).astype(o_ref.dtype)


def rms_matmul(x, g, w, *, out_dtype, tm, tn):
    s = x.shape[0]
    k, n = w.shape
    return pl.pallas_call(
        _rms_matmul_kernel,
        grid=(s // tm, n // tn),
        in_specs=[
            pl.BlockSpec((tm, k), lambda i, j: (i, 0)),
            pl.BlockSpec((1, k), lambda i, j: (0, 0)),
            pl.BlockSpec((k, tn), lambda i, j: (0, j)),
        ],
        out_specs=pl.BlockSpec((tm, tn), lambda i, j: (i, j)),
        out_shape=jax.ShapeDtypeStruct((s, n), out_dtype),
        scratch_shapes=[pltpu.VMEM((tm, k), BF16)],
        compiler_params=_params("parallel", "arbitrary"),
        name="rms_matmul",
    )(x, g.reshape(1, k), w)


def _matmul_residual_kernel(a_ref, w_ref, r_ref, o_ref):
    @pl.when(pl.program_id(1) == 0)
    def _():
        o_ref[...] = r_ref[...]

    o_ref[...] += _dot(a_ref[...], w_ref[...])


def matmul_residual(a, w, res, *, tm, tk):
    s, k = a.shape
    n = w.shape[1]
    return pl.pallas_call(
        _matmul_residual_kernel,
        grid=(s // tm, k // tk),
        in_specs=[
            pl.BlockSpec((tm, tk), lambda i, kk: (i, kk)),
            pl.BlockSpec((tk, n), lambda i, kk: (kk, 0)),
            pl.BlockSpec((tm, n), lambda i, kk: (i, 0)),
        ],
        out_specs=pl.BlockSpec((tm, n), lambda i, kk: (i, 0)),
        out_shape=jax.ShapeDtypeStruct((s, n), F32),
        compiler_params=_params("parallel", "arbitrary"),
        name="matmul_residual",
    )(a, w, res)


def _mla_prep_kernel(cq_ref, ckv_ref, sm_ref, cos_ref, sin_ref, qn_ref, kvn_ref, wq_ref, wkv_ref,
                     q_out, k_out, v_out):
    scale = MLA_QK ** -0.5 * LOG2_E
    hq = _rms(cq_ref[...].astype(F32), qn_ref[...]).astype(BF16)
    qa = _dot(hq, wq_ref[...])
    cos = cos_ref[...]
    sin = sin_ref[...]
    nope_w = MLA_HEADS * MLA_NOPE
    rope_w = MLA_HEADS * MLA_ROPE
    qr = qa[:, nope_w:nope_w + rope_w] * cos + qa[:, nope_w + rope_w:] * sin
    for h in range(MLA_HEADS):
        q_out[h, :, 0:MLA_NOPE] = (qa[:, h * MLA_NOPE:(h + 1) * MLA_NOPE] * scale).astype(BF16)
        q_out[h, :, MLA_NOPE:MLA_QK] = (qr[:, h * MLA_ROPE:(h + 1) * MLA_ROPE] * scale).astype(BF16)

    hkv = _rms(ckv_ref[...].astype(F32), kvn_ref[...]).astype(BF16)
    kva = _dot(hkv, wkv_ref[...])
    sm = sm_ref[...]
    kr = (sm[:, SM_KROPE:SM_KROPE + MLA_ROPE] * cos[:, 0:MLA_ROPE]
          + sm[:, SM_KROT:SM_KROT + MLA_ROPE] * sin[:, 0:MLA_ROPE]).astype(BF16)
    for h in range(MLA_HEADS):
        k_out[h, :, 0:MLA_NOPE] = kva[:, h * MLA_NOPE:(h + 1) * MLA_NOPE].astype(BF16)
        k_out[h, :, MLA_NOPE:MLA_QK] = kr
    v_out[...] = kva[:, nope_w:].astype(BF16)


def mla_prep(big, small, cos8, sin8, q_norm, kv_norm, wq, wkv, *, tm):
    s = big.shape[0]
    rope_w = MLA_HEADS * MLA_ROPE
    return pl.pallas_call(
        _mla_prep_kernel,
        grid=(s // tm,),
        in_specs=[
            pl.BlockSpec((tm, MLA_RANK), lambda i: (i, BIG_CQ // MLA_RANK)),
            pl.BlockSpec((tm, MLA_RANK), lambda i: (i, BIG_CKV // MLA_RANK)),
            pl.BlockSpec((tm, SM_W), lambda i: (i, 0)),
            pl.BlockSpec((tm, rope_w), lambda i: (i, 0)),
            pl.BlockSpec((tm, rope_w), lambda i: (i, 0)),
            pl.BlockSpec((1, MLA_RANK), lambda i: (0, 0)),
            pl.BlockSpec((1, MLA_RANK), lambda i: (0, 0)),
            pl.BlockSpec(wq.shape, lambda i: (0, 0)),
            pl.BlockSpec(wkv.shape, lambda i: (0, 0)),
        ],
        out_specs=[
            pl.BlockSpec((MLA_HEADS, tm, MLA_QK), lambda i: (0, i, 0)),
            pl.BlockSpec((MLA_HEADS, tm, MLA_QK), lambda i: (0, i, 0)),
            pl.BlockSpec((tm, MLA_HEADS * MLA_V), lambda i: (i, 0)),
        ],
        out_shape=[
            jax.ShapeDtypeStruct((MLA_HEADS, s, MLA_QK), BF16),
            jax.ShapeDtypeStruct((MLA_HEADS, s, MLA_QK), BF16),
            jax.ShapeDtypeStruct((s, MLA_HEADS * MLA_V), BF16),
        ],
        compiler_params=_params("parallel"),
        name="mla_prep",
    )(big, big, small, cos8, sin8, q_norm.reshape(1, -1), kv_norm.reshape(1, -1), wq, wkv)


_LANES = 128
_FLASH_HEADS = 2


def _flash_kernel(qi_ref, ki_ref, q_ref, k_ref, v_ref, o_ref, m_ref, l_ref, acc_ref, *, tq, tk):
    p = pl.program_id(1)
    qi = qi_ref[p]
    ki = ki_ref[p]
    n_chunks = tk // _LANES

    @pl.when(ki == 0)
    def _():
        m_ref[...] = jnp.full(m_ref.shape, NEG_BIG, F32)
        l_ref[...] = jnp.zeros(l_ref.shape, F32)
        acc_ref[...] = jnp.zeros(acc_ref.shape, F32)

    def step(masked):
        if masked:
            row = qi * tq + lax.broadcasted_iota(jnp.int32, (tq, tk), 0)
            col = ki * tk + lax.broadcasted_iota(jnp.int32, (tq, tk), 1)
            keep = col <= row
        for h in range(_FLASH_HEADS):
            s = _dot_nt(q_ref[h], k_ref[h])
            if masked:
                s = jnp.where(keep, s, NEG_BIG)
            chunks = [s[:, c * _LANES:(c + 1) * _LANES] for c in range(n_chunks)]
            m_loc = chunks[0]
            for ch in chunks[1:]:
                m_loc = jnp.maximum(m_loc, ch)
            m_prev = m_ref[h]
            m_new = jnp.maximum(m_prev, jnp.max(m_loc, axis=-1, keepdims=True))
            alpha = jnp.exp2(m_prev - m_new)
            probs = [jnp.exp2(ch - m_new) for ch in chunks]
            l_loc = probs[0]
            for pr in probs[1:]:
                l_loc = l_loc + pr
            l_ref[h] = alpha * l_ref[h] + l_loc
            pb = jnp.concatenate([pr.astype(BF16) for pr in probs], axis=1)
            acc_ref[h] = alpha * acc_ref[h] + _dot(pb, v_ref[:, h * MLA_V:(h + 1) * MLA_V])
            m_ref[h] = m_new

    crosses = (ki + 1) * tk - 1 > qi * tq

    @pl.when(crosses)
    def _():
        step(True)

    @pl.when(jnp.logical_not(crosses))
    def _():
        step(False)

    @pl.when(ki == ((qi + 1) * tq - 1) // tk)
    def _():
        for h in range(_FLASH_HEADS):
            l_row = jnp.sum(l_ref[h], axis=-1, keepdims=True)
            o_ref[:, h * MLA_V:(h + 1) * MLA_V] = (acc_ref[h] / l_row).astype(o_ref.dtype)


def flash_attention(qc, kc, v, *, tq, tk):
    s = qc.shape[1]
    nq = s // tq
    qi_list, ki_list = [], []
    for qi in range(nq):
        for ki in range(((qi + 1) * tq - 1) // tk + 1):
            qi_list.append(qi)
            ki_list.append(ki)
    qi_arr = jnp.asarray(np.array(qi_list, np.int32))
    ki_arr = jnp.asarray(np.array(ki_list, np.int32))
    g = _FLASH_HEADS
    grid_spec = pltpu.PrefetchScalarGridSpec(
        num_scalar_prefetch=2,
        grid=(MLA_HEADS // g, len(qi_list)),
        in_specs=[
            pl.BlockSpec((g, tq, MLA_QK), lambda h, p, qi, ki: (h, qi[p], 0)),
            pl.BlockSpec((g, tk, MLA_QK), lambda h, p, qi, ki: (h, ki[p], 0)),
            pl.BlockSpec((tk, g * MLA_V), lambda h, p, qi, ki: (ki[p], h)),
        ],
        out_specs=pl.BlockSpec((tq, g * MLA_V), lambda h, p, qi, ki: (qi[p], h)),
        scratch_shapes=[
            pltpu.VMEM((g, tq, _LANES), F32),
            pltpu.VMEM((g, tq, _LANES), F32),
            pltpu.VMEM((g, tq, MLA_V), F32),
        ],
    )
    return pl.pallas_call(
        functools.partial(_flash_kernel, tq=tq, tk=tk),
        grid_spec=grid_spec,
        out_shape=jax.ShapeDtypeStruct((s, MLA_HEADS * MLA_V), BF16),
        compiler_params=_params("parallel", "arbitrary"),
        name="flash_attention",
    )(qi_arr, ki_arr, qc, kc, v)


_GLA_LEVELS = 6
_GLA_E_BINCL = 0
_GLA_E_KSTATE = 1
_GLA_E_Q0 = 2
_GLA_E_K0 = 2 + _GLA_LEVELS
_GLA_E_BLAST = 2 + 2 * _GLA_LEVELS
_GLA_E_ROWS = CHUNK * _GLA_E_BLAST + 8


def _gla_constants():
    c = CHUNK
    i = np.arange(c)[:, None]
    t = np.arange(c)[None, :]
    sel = np.zeros((_GLA_E_ROWS, c), np.float32)
    sel[_GLA_E_BINCL * c:(_GLA_E_BINCL + 1) * c] = t <= i
    sel[_GLA_E_KSTATE * c:(_GLA_E_KSTATE + 1) * c] = t > i
    masks = np.zeros((_GLA_LEVELS + 1, c, c), np.float32)
    for lvl in range(_GLA_LEVELS):
        half = c >> (lvl + 1)
        blk = 2 * half
        pos = i % blk
        mid = (i // blk) * blk + half
        sel[(_GLA_E_Q0 + lvl) * c:(_GLA_E_Q0 + lvl + 1) * c] = (pos >= half) & (t >= mid) & (t <= i)
        sel[(_GLA_E_K0 + lvl) * c:(_GLA_E_K0 + lvl + 1) * c] = (pos < half) & (t > i) & (t <= mid - 1)
        j = np.arange(c)[None, :]
        masks[lvl] = (i // blk == j // blk) & (pos >= half) & (j % blk < half)
    sel[_GLA_E_BLAST * c:] = 1.0
    masks[_GLA_LEVELS] = np.eye(c)
    return sel, masks


def _gla_kernel(q_ref, k_ref, v_ref, r_ref, sm_ref, w2_ref, bias_ref, sel_ref, mask_ref, gn_ref,
                y_ref, st_ref):
    c = CHUNK

    @pl.when(pl.program_id(0) == 0)
    def _():
        st_ref[...] = jnp.zeros(st_ref.shape, F32)

    z = _dot_hp(sm_ref[:, SM_GLR:], w2_ref[...]) + bias_ref[...]
    gk = (jnp.minimum(z, 0.0) - jnp.log(1.0 + jnp.exp(-jnp.abs(z)))) * (1.0 / GLA_GATE_NORM)
    gh, gl = _split2(gk)
    sel = sel_ref[...]
    e_all = jnp.exp(_dot(sel, gh) + _dot(sel, gl))

    def e_blk(idx, cols):
        return e_all[idx * c:(idx + 1) * c, cols]

    for h in range(GLA_HEADS):
        kc = slice(h * GLA_DK, (h + 1) * GLA_DK)
        vc = slice(h * GLA_DV, (h + 1) * GLA_DV)
        qs = q_ref[:, kc].astype(F32) * (GLA_DK ** -0.5)
        kb = k_ref[:, kc]
        kf = kb.astype(F32)
        vb = v_ref[:, vc]
        a = mask_ref[_GLA_LEVELS] * _dot_nt(qs.astype(BF16), kb)
        for lvl in range(_GLA_LEVELS):
            ql = (qs * e_blk(_GLA_E_Q0 + lvl, kc)).astype(BF16)
            kl = (kf * e_blk(_GLA_E_K0 + lvl, kc)).astype(BF16)
            a = a + mask_ref[lvl] * _dot_nt(ql, kl)
        st = st_ref[h]
        qb = (qs * e_blk(_GLA_E_BINCL, kc)).astype(BF16)
        o = _dot_nt(qb, st.astype(BF16)) + _dot(a.astype(BF16), vb)
        kd = (kf * e_blk(_GLA_E_KSTATE, kc)).astype(BF16)
        e_last = e_all[_GLA_E_BLAST * c:_GLA_E_BLAST * c + 1, kc]
        st_ref[h] = st * e_last + _dot_tn(vb, kd)
        o = _rms(o, gn_ref[:, vc])
        y_ref[:, vc] = (o * _silu(r_ref[:, vc].astype(F32))).astype(y_ref.dtype)


def gla_chunk(big, small, w_gate2, gate_bias, out_norm):
    s = big.shape[0]
    c = CHUNK
    sel, masks = _gla_constants()
    kw = GLA_HEADS * GLA_DK
    vw = GLA_HEADS * GLA_DV
    w2_pad = jnp.zeros((SM_W - SM_GLR, kw), F32).at[0:GLA_GATE_RANK].set(w_gate2)
    return pl.pallas_call(
        _gla_kernel,
        grid=(s // c,),
        in_specs=[
            pl.BlockSpec((c, kw), lambda i: (i, BIG_GLA_Q // kw)),
            pl.BlockSpec((c, kw), lambda i: (i, BIG_GLA_K // kw)),
            pl.BlockSpec((c, vw), lambda i: (i, BIG_GLA_V // vw)),
            pl.BlockSpec((c, vw), lambda i: (i, BIG_GLA_R // vw)),
            pl.BlockSpec((c, SM_W), lambda i: (i, 0)),
            pl.BlockSpec((SM_W - SM_GLR, kw), lambda i: (0, 0)),
            pl.BlockSpec((1, kw), lambda i: (0, 0)),
            pl.BlockSpec(sel.shape, lambda i: (0, 0)),
            pl.BlockSpec(masks.shape, lambda i: (0, 0, 0)),
            pl.BlockSpec((1, vw), lambda i: (0, 0)),
        ],
        out_specs=pl.BlockSpec((c, vw), lambda i: (i, 0)),
        out_shape=jax.ShapeDtypeStruct((s, vw), BF16),
        scratch_shapes=[pltpu.VMEM((GLA_HEADS, GLA_DV, GLA_DK), F32)],
        compiler_params=_params("arbitrary"),
        name="gla_chunk",
    )(big, big, big, big, small, w2_pad, gate_bias.reshape(1, kw),
      jnp.asarray(sel, BF16), jnp.asarray(masks), out_norm.reshape(1, vw))


_GDN_HALO = 16
_GDN_CW = 2 * GDN_QK_HEADS * GDN_DK + GDN_V_HEADS * GDN_DV
_GDN_SLAB_B = SM_GDN_B - SM_GLR
_GDN_SLAB_A = SM_GDN_A - SM_GLR


def _gdn_constants():
    c = CHUNK
    i = np.arange(c)[:, None]
    j = np.arange(c)[None, :]
    cum = np.concatenate([(j <= i), (j > i)], axis=0).astype(np.float32)
    tri = np.stack([(i >= j), (i > j)]).astype(np.float32)
    return cum, tri


def _gdn_gates(sm_ref, alog_ref, dtb_ref):
    slab = sm_ref[:, SM_GLR:]
    beta = _sigmoid(slab)
    g = -jnp.exp(alog_ref[...]) * _softplus(slab + dtb_ref[...])
    return beta, g


def _gdn_prep_kernel(x_ref, halo_ref, sm_ref, cw_ref, alog_ref, dtb_ref, cum_ref, tri_ref,
                     qkv_out, n_out, qk_out, xs_ref):
    c = CHUNK
    kw = GDN_QK_HEADS * GDN_DK

    halo = halo_ref[...].astype(F32)
    xs_ref[0:_GDN_HALO, :] = jnp.where(pl.program_id(0) > 0, halo, 0.0)
    xs_ref[_GDN_HALO:, :] = x_ref[...].astype(F32)
    conv = cw_ref[0:1, :] * xs_ref[pl.ds(_GDN_HALO - 3, c), :]
    for tap in range(1, GDN_CONV):
        conv = conv + cw_ref[tap:tap + 1, :] * xs_ref[pl.ds(_GDN_HALO - 3 + tap, c), :]
    qkv = _silu(conv)
    qkv_out[:, 2 * kw:] = qkv[:, 2 * kw:].astype(BF16)

    beta, g = _gdn_gates(sm_ref, alog_ref, dtb_ref)
    low = cum_ref[0:c, :]
    causal = tri_ref[0]
    strict = tri_ref[1]

    kk, qk0 = [], []
    for hq in range(GDN_QK_HEADS):
        qr = qkv[:, hq * GDN_DK:(hq + 1) * GDN_DK]
        kr = qkv[:, kw + hq * GDN_DK:kw + (hq + 1) * GDN_DK]
        qb = (qr * lax.rsqrt(jnp.sum(qr * qr, axis=-1, keepdims=True) + EPS) * (GDN_DK ** -0.5)).astype(BF16)
        kb = (kr * lax.rsqrt(jnp.sum(kr * kr, axis=-1, keepdims=True) + EPS)).astype(BF16)
        qkv_out[:, hq * GDN_DK:(hq + 1) * GDN_DK] = qb
        qkv_out[:, kw + hq * GDN_DK:kw + (hq + 1) * GDN_DK] = kb
        kk.append(_dot_nt(kb, kb))
        qk0.append(_dot_nt(qb, kb))

    rep = GDN_V_HEADS // GDN_QK_HEADS
    for hv in range(GDN_V_HEADS):
        hq = hv // rep
        bcol = beta[:, _GDN_SLAB_B + hv:_GDN_SLAB_B + hv + 1]
        d = _dot_exact_lhs(low, g[:, _GDN_SLAB_A + hv:_GDN_SLAB_A + hv + 1] * strict)
        gam = causal * jnp.exp(d)
        n_out[0, hv] = strict * (bcol * gam * kk[hq])
        qk_out[0, hv] = (gam * qk0[hq]).astype(BF16)


def gdn_prep(big, small, conv_w, alog_slab, dtb_slab):
    s = big.shape[0]
    c = CHUNK
    nc = s // c
    cum, tri = _gdn_constants()
    halo_per_chunk = c // _GDN_HALO
    slab_w = SM_W - SM_GLR
    const2 = lambda i: (0, 0)
    hcc = (GDN_V_HEADS, c, c)
    return pl.pallas_call(
        _gdn_prep_kernel,
        grid=(nc,),
        in_specs=[
            pl.BlockSpec((c, _GDN_CW), lambda i: (i, BIG_GDN_QKV // _GDN_CW)),
            pl.BlockSpec((_GDN_HALO, _GDN_CW),
                         lambda i: (jnp.maximum(i * halo_per_chunk - 1, 0), BIG_GDN_QKV // _GDN_CW)),
            pl.BlockSpec((c, SM_W), lambda i: (i, 0)),
            pl.BlockSpec((GDN_CONV, _GDN_CW), const2),
            pl.BlockSpec((1, slab_w), const2),
            pl.BlockSpec((1, slab_w), const2),
            pl.BlockSpec(cum.shape, const2),
            pl.BlockSpec(tri.shape, lambda i: (0, 0, 0)),
        ],
        out_specs=[
            pl.BlockSpec((c, _GDN_CW), lambda i: (i, 0)),
            pl.BlockSpec((1,) + hcc, lambda i: (i, 0, 0, 0)),
            pl.BlockSpec((1,) + hcc, lambda i: (i, 0, 0, 0)),
        ],
        out_shape=[
            jax.ShapeDtypeStruct((s, _GDN_CW), BF16),
            jax.ShapeDtypeStruct((nc,) + hcc, F32),
            jax.ShapeDtypeStruct((nc,) + hcc, BF16),
        ],
        scratch_shapes=[pltpu.VMEM((_GDN_HALO + c, _GDN_CW), F32)],
        compiler_params=_params("parallel"),
        name="gdn_prep",
    )(big, big, small, conv_w, alog_slab, dtb_slab, jnp.asarray(cum, BF16), jnp.asarray(tri))


def _gdn_solve_kernel(n_ref, x_ref):
    c = CHUNK
    group = 8
    x_ref[...] = jnp.zeros(x_ref.shape, F32)
    col = lax.broadcasted_iota(jnp.int32, x_ref.shape[1:], 0)

    def row(i, carry):
        def kgroup(kg, acc):
            k0 = pl.multiple_of(kg * group, group)
            n8 = n_ref[i, pl.ds(k0, group), :]
            for r in range(group):
                acc = acc - n8[r:r + 1, :] * x_ref[k0 + r]
            return acc

        acc0 = jnp.where(col == i, 1.0, 0.0)
        n_groups = lax.shift_right_logical(i + (group - 1), 3)
        x_ref[i] = lax.fori_loop(0, n_groups, kgroup, acc0)
        return carry

    lax.fori_loop(0, c, row, 0)


def gdn_solve(n_t):
    c = CHUNK
    p = n_t.shape[2]
    pb = min(128, p)
    spec = pl.BlockSpec((c, c, pb), lambda i: (0, 0, i))
    return pl.pallas_call(
        _gdn_solve_kernel,
        grid=(p // pb,),
        in_specs=[spec],
        out_specs=spec,
        out_shape=jax.ShapeDtypeStruct(n_t.shape, F32),
        compiler_params=_params("parallel"),
        name="gdn_solve",
    )(n_t)


def _gdn_kernel(qkv_ref, z_ref, sm_ref, alog_ref, dtb_ref, cum_ref, x_ref, qk_ref, gn_ref,
                y_ref, st_ref):
    c = CHUNK
    kw = GDN_QK_HEADS * GDN_DK

    @pl.when(pl.program_id(0) == 0)
    def _():
        st_ref[...] = jnp.zeros(st_ref.shape, F32)

    beta, g = _gdn_gates(sm_ref, alog_ref, dtb_ref)
    cums = _dot_exact_lhs(cum_ref[...], g)
    e_g = jnp.exp(cums[0:c])
    e_gk = jnp.exp(cums[c:2 * c])
    e_last = e_g[c - 1:c, :]

    rep = GDN_V_HEADS // GDN_QK_HEADS
    for hv in range(GDN_V_HEADS):
        hq = hv // rep
        vc = slice(hv * GDN_DV, (hv + 1) * GDN_DV)
        lane = _GDN_SLAB_A + hv
        qn = qkv_ref[:, hq * GDN_DK:(hq + 1) * GDN_DK].astype(F32)
        kn = qkv_ref[:, kw + hq * GDN_DK:kw + (hq + 1) * GDN_DK].astype(F32)
        v = qkv_ref[:, 2 * kw + hv * GDN_DV:2 * kw + (hv + 1) * GDN_DV].astype(F32)
        bcol = beta[:, _GDN_SLAB_B + hv:_GDN_SLAB_B + hv + 1]
        egc = e_g[:, lane:lane + 1]
        rhs = jnp.concatenate([(bcol * egc) * kn, bcol * v], axis=1)
        wu = _dot_hp(x_ref[0, hv], rhs)
        w = wu[:, 0:GDN_DK]
        u0 = wu[:, GDN_DK:]
        st = st_ref[hv]
        stb = st.astype(BF16)
        u = u0 - _dot(w.astype(BF16), stb)
        ub = u.astype(BF16)
        o = _dot((qn * egc).astype(BF16), stb) + _dot(qk_ref[0, hv], ub)
        kd = (kn * e_gk[:, lane:lane + 1]).astype(BF16)
        st_ref[hv] = st * e_last[:, lane:lane + 1] + _dot_tn(kd, ub)
        o = _rms(o, gn_ref[:, vc])
        y_ref[:, vc] = (o * _silu(z_ref[:, vc].astype(F32))).astype(y_ref.dtype)


def gdn_chunk(big, small, conv_w, a_log, dt_bias, out_norm):
    s = big.shape[0]
    c = CHUNK
    nc = s // c
    vw = GDN_V_HEADS * GDN_DV
    cum, _ = _gdn_constants()
    slab_w = SM_W - SM_GLR
    alog_slab = jnp.zeros((1, slab_w), F32).at[0, _GDN_SLAB_A:_GDN_SLAB_A + GDN_V_HEADS].set(a_log)
    dtb_slab = jnp.zeros((1, slab_w), F32).at[0, _GDN_SLAB_A:_GDN_SLAB_A + GDN_V_HEADS].set(dt_bias)

    qkvn, nmat, qkm = gdn_prep(big, small, conv_w, alog_slab, dtb_slab)
    n_t = nmat.reshape(nc * GDN_V_HEADS, c * c).T.reshape(c, c, nc * GDN_V_HEADS)
    x_t = gdn_solve(n_t)
    xmat = x_t.reshape(c * c, nc * GDN_V_HEADS).T.reshape(nc, GDN_V_HEADS, c, c)

    const2 = lambda i: (0, 0)
    hcc = (1, GDN_V_HEADS, c, c)
    return pl.pallas_call(
        _gdn_kernel,
        grid=(nc,),
        in_specs=[
            pl.BlockSpec((c, _GDN_CW), lambda i: (i, 0)),
            pl.BlockSpec((c, vw), lambda i: (i, BIG_GDN_Z // vw)),
            pl.BlockSpec((c, SM_W), lambda i: (i, 0)),
            pl.BlockSpec((1, slab_w), const2),
            pl.BlockSpec((1, slab_w), const2),
            pl.BlockSpec(cum.shape, const2),
            pl.BlockSpec(hcc, lambda i: (i, 0, 0, 0)),
            pl.BlockSpec(hcc, lambda i: (i, 0, 0, 0)),
            pl.BlockSpec((1, vw), const2),
        ],
        out_specs=pl.BlockSpec((c, vw), lambda i: (i, 0)),
        out_shape=jax.ShapeDtypeStruct((s, vw), BF16),
        scratch_shapes=[pltpu.VMEM((GDN_V_HEADS, GDN_DK, GDN_DV), F32)],
        compiler_params=_params("arbitrary"),
        name="gdn_chunk",
    )(qkvn, big, small, alog_slab, dtb_slab, jnp.asarray(cum, BF16), xmat, qkm,
      out_norm.reshape(1, vw))


def _gated_mix_kernel(y0_ref, y1_ref, y2_ref, w0_ref, w1_ref, w2_ref, g0_ref, g1_ref, g2_ref, o_ref):
    acc = _sigmoid(g0_ref[...].astype(F32)) * _dot(y0_ref[...], w0_ref[...])
    acc = acc + _sigmoid(g1_ref[...].astype(F32)) * _dot(y1_ref[...], w1_ref[...])
    acc = acc + _sigmoid(g2_ref[...].astype(F32)) * _dot(y2_ref[...], w2_ref[...])
    o_ref[...] = acc.astype(o_ref.dtype)


def gated_mix(ys, ws, big, *, tm, tn):
    s, k = ys[0].shape
    n = ws[0].shape[1]
    y_spec = pl.BlockSpec((tm, k), lambda i, j: (i, 0))
    w_spec = pl.BlockSpec((k, tn), lambda i, j: (0, j))

    def gate_spec(b):
        base = (BIG_MERGE + b * n) // tn
        return pl.BlockSpec((tm, tn), lambda i, j: (i, base + j))

    return pl.pallas_call(
        _gated_mix_kernel,
        grid=(s // tm, n // tn),
        in_specs=[y_spec] * 3 + [w_spec] * 3 + [gate_spec(0), gate_spec(1), gate_spec(2)],
        out_specs=pl.BlockSpec((tm, tn), lambda i, j: (i, j)),
        out_shape=jax.ShapeDtypeStruct((s, n), BF16),
        compiler_params=_params("parallel", "parallel"),
        name="gated_mix",
    )(*ys, *ws, big, big, big)


_FFN_HALO = 8


def _ffn_up_kernel(x_ref, xh_ref, g_ref, wg_ref, wu_ref, cw_ref, cb_ref, o_ref, h_ref, hh_ref, gs_ref):
    tm = x_ref.shape[0]

    @pl.when(pl.program_id(1) == 0)
    def _():
        h_ref[...] = _rms(x_ref[...], g_ref[...]).astype(BF16)
        hh_ref[...] = _rms(xh_ref[...], g_ref[...]).astype(BF16)

    gate_halo = _dot(hh_ref[...], wg_ref[...])
    gs_ref[0:_FFN_HALO, :] = jnp.where(pl.program_id(0) > 0, gate_halo, 0.0)
    gs_ref[_FFN_HALO:, :] = _dot(h_ref[...], wg_ref[...])
    conv = cb_ref[...] + cw_ref[0:1, :] * gs_ref[pl.ds(_FFN_HALO - 2, tm), :]
    for tap in range(1, FFN_CONV):
        conv = conv + cw_ref[tap:tap + 1, :] * gs_ref[pl.ds(_FFN_HALO - 2 + tap, tm), :]
    up = _dot(h_ref[...], wu_ref[...])
    o_ref[...] = (_silu(conv) * up).astype(o_ref.dtype)


def ffn_up(x, g, w_up, conv_w, conv_b, *, tm, tn):
    s, k = x.shape
    halo_per_tile = tm // _FFN_HALO
    n_tiles = D_FF // tn
    return pl.pallas_call(
        _ffn_up_kernel,
        grid=(s // tm, n_tiles),
        in_specs=[
            pl.BlockSpec((tm, k), lambda i, j: (i, 0)),
            pl.BlockSpec((_FFN_HALO, k), lambda i, j: (jnp.maximum(i * halo_per_tile - 1, 0), 0)),
            pl.BlockSpec((1, k), lambda i, j: (0, 0)),
            pl.BlockSpec((k, tn), lambda i, j: (0, j)),
            pl.BlockSpec((k, tn), lambda i, j: (0, n_tiles + j)),
            pl.BlockSpec((FFN_CONV, tn), lambda i, j: (0, j)),
            pl.BlockSpec((1, tn), lambda i, j: (0, j)),
        ],
        out_specs=pl.BlockSpec((tm, tn), lambda i, j: (i, j)),
        out_shape=jax.ShapeDtypeStruct((s, D_FF), BF16),
        scratch_shapes=[
            pltpu.VMEM((tm, k), BF16),
            pltpu.VMEM((_FFN_HALO, k), BF16),
            pltpu.VMEM((_FFN_HALO + tm, tn), F32),
        ],
        compiler_params=_params("parallel", "arbitrary"),
        name="ffn_up",
    )(x, x, g.reshape(1, k), w_up, w_up, conv_w, conv_b.reshape(1, -1))


def _rms_kernel(x_ref, g_ref, o_ref):
    o_ref[...] = _rms(x_ref[...], g_ref[...])


def rms_only(x, g, *, tm):
    s, k = x.shape
    return pl.pallas_call(
        _rms_kernel,
        grid=(s // tm,),
        in_specs=[pl.BlockSpec((tm, k), lambda i: (i, 0)), pl.BlockSpec((1, k), lambda i: (0, 0))],
        out_specs=pl.BlockSpec((tm, k), lambda i: (i, 0)),
        out_shape=jax.ShapeDtypeStruct((s, k), F32),
        compiler_params=_params("parallel"),
        name="final_rms",
    )(x, g.reshape(1, k))


def _in_proj_columns():
    widths = (MLA_RANK, MLA_RANK, MLA_ROPE, 512, 512, 1024, GLA_GATE_RANK, 1024, 2048,
              GDN_V_HEADS, GDN_V_HEADS, 1024, 3 * D_MODEL)
    names = ("c_q", "c_kv", "k_rope", "gla_q", "gla_k", "gla_v", "gla_g", "gla_r", "gdn_qkv",
             "gdn_b", "gdn_a", "gdn_z", "merge")
    off, cols = 0, {}
    for nme, wd in zip(names, widths):
        cols[nme] = (off, off + wd)
        off += wd
    return cols


def _prep_in_proj(w):
    cols = _in_proj_columns()

    def take(name):
        a, b = cols[name]
        return w[:, a:b]

    kr = take("k_rope")
    half = MLA_ROPE // 2
    kr_rot = jnp.concatenate([-kr[:, half:], kr[:, :half]], axis=1)
    small = jnp.concatenate(
        [kr, kr_rot, take("gla_g"), take("gdn_b"), take("gdn_a"),
         jnp.zeros((w.shape[0], SM_W - SM_GDN_A - GDN_V_HEADS), w.dtype)], axis=1)
    big = jnp.concatenate([take(n).astype(BF16) for n in
                           ("c_q", "c_kv", "gla_q", "gla_k", "gla_v", "gla_r", "gdn_qkv", "gdn_z", "merge")],
                          axis=1)
    return big, small.astype(BF16)


def _prep_mla_weights(w_uq, w_ukv):
    half = MLA_ROPE // 2
    wq = w_uq.reshape(MLA_RANK, MLA_HEADS, MLA_QK)
    nope = wq[:, :, :MLA_NOPE].reshape(MLA_RANK, -1)
    rope = wq[:, :, MLA_NOPE:]
    rot = jnp.concatenate([-rope[:, :, half:], rope[:, :, :half]], axis=2)
    wq_ext = jnp.concatenate([nope, rope.reshape(MLA_RANK, -1), rot.reshape(MLA_RANK, -1)], axis=1)
    wkv = w_ukv.reshape(MLA_RANK, MLA_HEADS, MLA_NOPE + MLA_V)
    wkv_ext = jnp.concatenate([wkv[:, :, :MLA_NOPE].reshape(MLA_RANK, -1),
                               wkv[:, :, MLA_NOPE:].reshape(MLA_RANK, -1)], axis=1)
    return wq_ext.astype(BF16), wkv_ext.astype(BF16)


def _rope_tables(positions):
    inv = ROPE_THETA ** (-jnp.arange(0, MLA_ROPE, 2, dtype=F32) / MLA_ROPE)
    ang = positions.astype(F32)[0][:, None] * inv
    cos = jnp.tile(jnp.cos(ang), (1, 2 * MLA_HEADS))
    sin = jnp.tile(jnp.sin(ang), (1, 2 * MLA_HEADS))
    return cos, sin


def _tile(s, pref):
    t = pref
    while s % t:
        t //= 2
    return t


def kernel(x, positions, attn_norm, w_in, mla_q_norm, mla_kv_norm, mla_w_uq, mla_w_ukv, gla_w_gate2,
           gla_gate_bias, gla_out_norm, gdn_conv_w, gdn_a_log, gdn_dt_bias, gdn_out_norm, w_branch_mla,
           w_branch_gla, w_branch_gdn, w_out, ffn_norm, ffn_w_up, ffn_conv_w, ffn_conv_b, ffn_w_down,
           final_norm):
    b, s, d = x.shape
    assert b == 1 and d == D_MODEL and s % CHUNK == 0
    depth = w_in.shape[0]
    xs = x.reshape(s, d)
    cos8, sin8 = _rope_tables(positions)
    tm = _tile(s, 1024)
    tm_small = _tile(s, 512)
    tq = _tile(s, 1024)

    for l in range(depth):
        w_big, w_small = _prep_in_proj(w_in[l])
        big = rms_matmul(xs, attn_norm[l], w_big, out_dtype=BF16, tm=tm, tn=1024)
        small = rms_matmul(xs, attn_norm[l], w_small, out_dtype=F32, tm=tm, tn=SM_W)

        wq, wkv = _prep_mla_weights(mla_w_uq[l], mla_w_ukv[l])
        qc, kc, v = mla_prep(big, small, cos8, sin8, mla_q_norm[l], mla_kv_norm[l], wq, wkv, tm=tm_small)
        y_mla = flash_attention(qc, kc, v, tq=tq, tk=tq)
        y_gla = gla_chunk(big, small, gla_w_gate2[l], gla_gate_bias[l], gla_out_norm[l])
        y_gdn = gdn_chunk(big, small, gdn_conv_w[l], gdn_a_log[l], gdn_dt_bias[l], gdn_out_norm[l])

        mixed = gated_mix((y_mla, y_gla, y_gdn),
                          (w_branch_mla[l].astype(BF16), w_branch_gla[l].astype(BF16),
                           w_branch_gdn[l].astype(BF16)), big, tm=tm, tn=512)
        xs = matmul_residual(mixed, w_out[l].astype(BF16), xs, tm=tm, tk=512)

        act = ffn_up(xs, ffn_norm[l], ffn_w_up[l].astype(BF16), ffn_conv_w[l], ffn_conv_b[l],
                     tm=tm, tn=512)
        xs = matmul_residual(act, ffn_w_down[l].astype(BF16), xs, tm=tm, tk=512)

    out = rms_only(xs, final_norm, tm=tm_small)
    return out.reshape(b, s, d)
```

```python
import functools

import numpy as np
import jax
import jax.numpy as jnp
from jax import lax
from jax.experimental import pallas as pl
from jax.experimental.pallas import tpu as pltpu

F32 = jnp.float32
BF16 = jnp.bfloat16

EPS = 1e-6
D_MODEL = 2048
MLA_HEADS = 8
MLA_NOPE = 128
MLA_ROPE = 64
MLA_V = 128
MLA_QK = MLA_NOPE + MLA_ROPE
MLA_RANK = 512
ROPE_THETA = 10000.0
GLA_HEADS = 4
GLA_DK = 128
GLA_DV = 256
GLA_GATE_RANK = 16
GLA_GATE_NORM = 16.0
GDN_QK_HEADS = 4
GDN_V_HEADS = 8
GDN_DK = 128
GDN_DV = 128
GDN_CONV = 4
CHUNK = 64
D_FF = 5632
FFN_CONV = 3

BIG_CQ = 0
BIG_CKV = 512
BIG_GLA_Q = 1024
BIG_GLA_K = 1536
BIG_GLA_V = 2048
BIG_GLA_R = 3072
BIG_GDN_QKV = 4096
BIG_GDN_Z = 6144
BIG_MERGE = 7168
BIG_W = 13312
SM_KROPE = 0
SM_KROT = 64
SM_GLR = 128
SM_GDN_B = 144
SM_GDN_A = 152
SM_W = 256

VMEM_LIMIT_BYTES = 56 * 1024 * 1024
NEG_BIG = -1e30
LOG2_E = 1.4426950408889634


def _params(*semantics):
    return pltpu.CompilerParams(dimension_semantics=semantics, vmem_limit_bytes=VMEM_LIMIT_BYTES)


def _dot(a, b):
    return jnp.dot(a, b, preferred_element_type=F32)


def _dot_nt(a, b):
    return lax.dot_general(a, b, (((1,), (1,)), ((), ())), preferred_element_type=F32)


def _dot_tn(a, b):
    return lax.dot_general(a, b, (((0,), (0,)), ((), ())), preferred_element_type=F32)


def _split2(x):
    hi = x.astype(BF16)
    lo = (x - hi.astype(F32)).astype(BF16)
    return hi, lo


def _split3(x):
    hi = x.astype(BF16)
    r = x - hi.astype(F32)
    mid = r.astype(BF16)
    lo = (r - mid.astype(F32)).astype(BF16)
    return hi, mid, lo


def _dot_hp(a, b):
    ah, al = _split2(a)
    bh, bl = _split2(b)
    return _dot(ah, bh) + (_dot(ah, bl) + _dot(al, bh))


def _dot_exact_lhs(m_bf16, x):
    h, m, l = _split3(x)
    return _dot(m_bf16, h) + (_dot(m_bf16, m) + _dot(m_bf16, l))


def _rms(xf, g):
    return xf * lax.rsqrt(jnp.mean(xf * xf, axis=-1, keepdims=True) + EPS) * g


def _sigmoid(x):
    return 1.0 / (1.0 + jnp.exp(-x))


def _silu(x):
    return x * _sigmoid(x)


def _softplus(x):
    return jnp.maximum(x, 0.0) + jnp.log(1.0 + jnp.exp(-jnp.abs(x)))


def _rms_matmul_kernel(x_ref, g_ref, w_ref, o_ref, h_ref):
    @pl.when(pl.program_id(1) == 0)
    def _():
        h_ref[...] = _rms(x_ref[...].astype(F32), g_ref[...]).astype(BF16)

    o_ref[...] = _dot(h_ref[...], w_ref[...]).astype(o_ref.dtype)


def rms_matmul(x, g, w, *, out_dtype, tm, tn):
    s = x.shape[0]
    k, n = w.shape
    return pl.pallas_call(
        _rms_matmul_kernel,
        grid=(s // tm, n // tn),
        in_specs=[
            pl.BlockSpec((tm, k), lambda i, j: (i, 0)),
            pl.BlockSpec((1, k), lambda i, j: (0, 0)),
            pl.BlockSpec((k, tn), lambda i, j: (0, j)),
        ],
        out_specs=pl.BlockSpec((tm, tn), lambda i, j: (i, j)),
        out_shape=jax.ShapeDtypeStruct((s, n), out_dtype),
        scratch_shapes=[pltpu.VMEM((tm, k), BF16)],
        compiler_params=_params("parallel", "arbitrary"),
        name="rms_matmul",
    )(x, g.reshape(1, k), w)


def _matmul_residual_kernel(a_ref, w_ref, r_ref, o_ref):
    @pl.when(pl.program_id(1) == 0)
    def _():
        o_ref[...] = r_ref[...]

    o_ref[...] += _dot(a_ref[...], w_ref[...])


def matmul_residual(a, w, res, *, tm, tk):
    s, k = a.shape
    n = w.shape[1]
    return pl.pallas_call(
        _matmul_residual_kernel,
        grid=(s // tm, k // tk),
        in_specs=[
            pl.BlockSpec((tm, tk), lambda i, kk: (i, kk)),
            pl.BlockSpec((tk, n), lambda i, kk: (kk, 0)),
            pl.BlockSpec((tm, n), lambda i, kk: (i, 0)),
        ],
        out_specs=pl.BlockSpec((tm, n), lambda i, kk: (i, 0)),
        out_shape=jax.ShapeDtypeStruct((s, n), F32),
        compiler_params=_params("parallel", "arbitrary"),
        name="matmul_residual",
    )(a, w, res)


def _mla_prep_kernel(cq_ref, ckv_ref, sm_ref, cos_ref, sin_ref, qn_ref, kvn_ref, wq_ref, wkv_ref,
                     q_out, k_out, v_out):
    scale = MLA_QK ** -0.5 * LOG2_E
    hq = _rms(cq_ref[...].astype(F32), qn_ref[...]).astype(BF16)
    qa = _dot(hq, wq_ref[...])
    cos = cos_ref[...]
    sin = sin_ref[...]
    nope_w = MLA_HEADS * MLA_NOPE
    rope_w = MLA_HEADS * MLA_ROPE
    qr = qa[:, nope_w:nope_w + rope_w] * cos + qa[:, nope_w + rope_w:] * sin
    for h in range(MLA_HEADS):
        q_out[h, :, 0:MLA_NOPE] = (qa[:, h * MLA_NOPE:(h + 1) * MLA_NOPE] * scale).astype(BF16)
        q_out[h, :, MLA_NOPE:MLA_QK] = (qr[:, h * MLA_ROPE:(h + 1) * MLA_ROPE] * scale).astype(BF16)

    hkv = _rms(ckv_ref[...].astype(F32), kvn_ref[...]).astype(BF16)
    kva = _dot(hkv, wkv_ref[...])
    sm = sm_ref[...]
    kr = (sm[:, SM_KROPE:SM_KROPE + MLA_ROPE] * cos[:, 0:MLA_ROPE]
          + sm[:, SM_KROT:SM_KROT + MLA_ROPE] * sin[:, 0:MLA_ROPE]).astype(BF16)
    for h in range(MLA_HEADS):
        k_out[h, :, 0:MLA_NOPE] = kva[:, h * MLA_NOPE:(h + 1) * MLA_NOPE].astype(BF16)
        k_out[h, :, MLA_NOPE:MLA_QK] = kr
    v_out[...] = kva[:, nope_w:].astype(BF16)


def mla_prep(big, small, cos8, sin8, q_norm, kv_norm, wq, wkv, *, tm):
    s = big.shape[0]
    rope_w = MLA_HEADS * MLA_ROPE
    return pl.pallas_call(
        _mla_prep_kernel,
        grid=(s // tm,),
        in_specs=[
            pl.BlockSpec((tm, MLA_RANK), lambda i: (i, BIG_CQ // MLA_RANK)),
            pl.BlockSpec((tm, MLA_RANK), lambda i: (i, BIG_CKV // MLA_RANK)),
            pl.BlockSpec((tm, SM_W), lambda i: (i, 0)),
            pl.BlockSpec((tm, rope_w), lambda i: (i, 0)),
            pl.BlockSpec((tm, rope_w), lambda i: (i, 0)),
            pl.BlockSpec((1, MLA_RANK), lambda i: (0, 0)),
            pl.BlockSpec((1, MLA_RANK), lambda i: (0, 0)),
            pl.BlockSpec(wq.shape, lambda i: (0, 0)),
            pl.BlockSpec(wkv.shape, lambda i: (0, 0)),
        ],
        out_specs=[
            pl.BlockSpec((MLA_HEADS, tm, MLA_QK), lambda i: (0, i, 0)),
            pl.BlockSpec((MLA_HEADS, tm, MLA_QK), lambda i: (0, i, 0)),
            pl.BlockSpec((tm, MLA_HEADS * MLA_V), lambda i: (i, 0)),
        ],
        out_shape=[
            jax.ShapeDtypeStruct((MLA_HEADS, s, MLA_QK), BF16),
            jax.ShapeDtypeStruct((MLA_HEADS, s, MLA_QK), BF16),
            jax.ShapeDtypeStruct((s, MLA_HEADS * MLA_V), BF16),
        ],
        compiler_params=_params("parallel"),
        name="mla_prep",
    )(big, big, small, cos8, sin8, q_norm.reshape(1, -1), kv_norm.reshape(1, -1), wq, wkv)


_LANES = 128
_FLASH_HEADS = 2


def _flash_kernel(qi_ref, ki_ref, q_ref, k_ref, v_ref, o_ref, m_ref, l_ref, acc_ref, *, tq, tk):
    p = pl.program_id(1)
    qi = qi_ref[p]
    ki = ki_ref[p]
    n_chunks = tk // _LANES

    @pl.when(ki == 0)
    def _():
        m_ref[...] = jnp.full(m_ref.shape, NEG_BIG, F32)
        l_ref[...] = jnp.zeros(l_ref.shape, F32)
        acc_ref[...] = jnp.zeros(acc_ref.shape, F32)

    def step(masked):
        if masked:
            row = qi * tq + lax.broadcasted_iota(jnp.int32, (tq, tk), 0)
            col = ki * tk + lax.broadcasted_iota(jnp.int32, (tq, tk), 1)
            keep = col <= row
        for h in range(_FLASH_HEADS):
            s = _dot_nt(q_ref[h], k_ref[h])
            if masked:
                s = jnp.where(keep, s, NEG_BIG)
            chunks = [s[:, c * _LANES:(c + 1) * _LANES] for c in range(n_chunks)]
            m_loc = chunks[0]
            for ch in chunks[1:]:
                m_loc = jnp.maximum(m_loc, ch)
            m_prev = m_ref[h]
            m_new = jnp.maximum(m_prev, jnp.max(m_loc, axis=-1, keepdims=True))
            alpha = jnp.exp2(m_prev - m_new)
            probs = [jnp.exp2(ch - m_new) for ch in chunks]
            l_loc = probs[0]
            for pr in probs[1:]:
                l_loc = l_loc + pr
            l_ref[h] = alpha * l_ref[h] + l_loc
            pb = jnp.concatenate([pr.astype(BF16) for pr in probs], axis=1)
            acc_ref[h] = alpha * acc_ref[h] + _dot(pb, v_ref[:, h * MLA_V:(h + 1) * MLA_V])
            m_ref[h] = m_new

    crosses = (ki + 1) * tk - 1 > qi * tq

    @pl.when(crosses)
    def _():
        step(True)

    @pl.when(jnp.logical_not(crosses))
    def _():
        step(False)

    @pl.when(ki == ((qi + 1) * tq - 1) // tk)
    def _():
        for h in range(_FLASH_HEADS):
            l_row = jnp.sum(l_ref[h], axis=-1, keepdims=True)
            o_ref[:, h * MLA_V:(h + 1) * MLA_V] = (acc_ref[h] / l_row).astype(o_ref.dtype)


def flash_attention(qc, kc, v, *, tq, tk):
    s = qc.shape[1]
    nq = s // tq
    qi_list, ki_list = [], []
    for qi in range(nq):
        for ki in range(((qi + 1) * tq - 1) // tk + 1):
            qi_list.append(qi)
            ki_list.append(ki)
    qi_arr = jnp.asarray(np.array(qi_list, np.int32))
    ki_arr = jnp.asarray(np.array(ki_list, np.int32))
    g = _FLASH_HEADS
    grid_spec = pltpu.PrefetchScalarGridSpec(
        num_scalar_prefetch=2,
        grid=(MLA_HEADS // g, len(qi_list)),
        in_specs=[
            pl.BlockSpec((g, tq, MLA_QK), lambda h, p, qi, ki: (h, qi[p], 0)),
            pl.BlockSpec((g, tk, MLA_QK), lambda h, p, qi, ki: (h, ki[p], 0)),
            pl.BlockSpec((tk, g * MLA_V), lambda h, p, qi, ki: (ki[p], h)),
        ],
        out_specs=pl.BlockSpec((tq, g * MLA_V), lambda h, p, qi, ki: (qi[p], h)),
        scratch_shapes=[
            pltpu.VMEM((g, tq, _LANES), F32),
            pltpu.VMEM((g, tq, _LANES), F32),
            pltpu.VMEM((g, tq, MLA_V), F32),
        ],
    )
    return pl.pallas_call(
        functools.partial(_flash_kernel, tq=tq, tk=tk),
        grid_spec=grid_spec,
        out_shape=jax.ShapeDtypeStruct((s, MLA_HEADS * MLA_V), BF16),
        compiler_params=_params("parallel", "arbitrary"),
        name="flash_attention",
    )(qi_arr, ki_arr, qc, kc, v)


_SCAN_CHUNKS_PER_STEP = 4
_GLA_LEVELS = 6
_GLA_E_BINCL = 0
_GLA_E_KSTATE = 1
_GLA_E_Q0 = 2
_GLA_E_K0 = 2 + _GLA_LEVELS
_GLA_E_BLAST = 2 + 2 * _GLA_LEVELS
_GLA_E_ROWS = CHUNK * _GLA_E_BLAST + 8


def _gla_constants():
    c = CHUNK
    i = np.arange(c)[:, None]
    t = np.arange(c)[None, :]
    sel = np.zeros((_GLA_E_ROWS, c), np.float32)
    sel[_GLA_E_BINCL * c:(_GLA_E_BINCL + 1) * c] = t <= i
    sel[_GLA_E_KSTATE * c:(_GLA_E_KSTATE + 1) * c] = t > i
    masks = np.zeros((_GLA_LEVELS + 1, c, c), np.float32)
    for lvl in range(_GLA_LEVELS):
        half = c >> (lvl + 1)
        blk = 2 * half
        pos = i % blk
        mid = (i // blk) * blk + half
        sel[(_GLA_E_Q0 + lvl) * c:(_GLA_E_Q0 + lvl + 1) * c] = (pos >= half) & (t >= mid) & (t <= i)
        sel[(_GLA_E_K0 + lvl) * c:(_GLA_E_K0 + lvl + 1) * c] = (pos < half) & (t > i) & (t <= mid - 1)
        j = np.arange(c)[None, :]
        masks[lvl] = (i // blk == j // blk) & (pos >= half) & (j % blk < half)
    sel[_GLA_E_BLAST * c:] = 1.0
    masks[_GLA_LEVELS] = np.eye(c)
    return sel, masks


def _gla_kernel(q_ref, k_ref, v_ref, r_ref, sm_ref, w2_ref, bias_ref, sel_ref, mask_ref, gn_ref,
                y_ref, st_ref):
    c = CHUNK

    @pl.when(pl.program_id(0) == 0)
    def _():
        st_ref[...] = jnp.zeros(st_ref.shape, F32)

    sel = sel_ref[...]
    for sub in range(q_ref.shape[0] // c):
        rows = slice(sub * c, (sub + 1) * c)
        z = _dot_hp(sm_ref[rows, SM_GLR:], w2_ref[...]) + bias_ref[...]
        gk = (jnp.minimum(z, 0.0) - jnp.log(1.0 + jnp.exp(-jnp.abs(z)))) * (1.0 / GLA_GATE_NORM)
        gh, gl = _split2(gk)
        e_all = jnp.exp(_dot(sel, gh) + _dot(sel, gl))

        def e_blk(idx, cols, e_all=e_all):
            return e_all[idx * c:(idx + 1) * c, cols]

        for h in range(GLA_HEADS):
            kc = slice(h * GLA_DK, (h + 1) * GLA_DK)
            vc = slice(h * GLA_DV, (h + 1) * GLA_DV)
            qs = q_ref[rows, kc].astype(F32) * (GLA_DK ** -0.5)
            kb = k_ref[rows, kc]
            kf = kb.astype(F32)
            vb = v_ref[rows, vc]
            a = mask_ref[_GLA_LEVELS] * _dot_nt(qs.astype(BF16), kb)
            for lvl in range(_GLA_LEVELS):
                ql = (qs * e_blk(_GLA_E_Q0 + lvl, kc)).astype(BF16)
                kl = (kf * e_blk(_GLA_E_K0 + lvl, kc)).astype(BF16)
                a = a + mask_ref[lvl] * _dot_nt(ql, kl)
            st = st_ref[h]
            qb = (qs * e_blk(_GLA_E_BINCL, kc)).astype(BF16)
            o = _dot_nt(qb, st.astype(BF16)) + _dot(a.astype(BF16), vb)
            kd = (kf * e_blk(_GLA_E_KSTATE, kc)).astype(BF16)
            e_last = e_all[_GLA_E_BLAST * c:_GLA_E_BLAST * c + 1, kc]
            st_ref[h] = st * e_last + _dot_tn(vb, kd)
            o = _rms(o, gn_ref[:, vc])
            y_ref[rows, vc] = (o * _silu(r_ref[rows, vc].astype(F32))).astype(y_ref.dtype)


def gla_chunk(big, small, w_gate2, gate_bias, out_norm):
    s = big.shape[0]
    c = CHUNK
    sel, masks = _gla_constants()
    kw = GLA_HEADS * GLA_DK
    vw = GLA_HEADS * GLA_DV
    w2_pad = jnp.zeros((SM_W - SM_GLR, kw), F32).at[0:GLA_GATE_RANK].set(w_gate2)
    c = _SCAN_CHUNKS_PER_STEP * CHUNK
    assert s % c == 0
    return pl.pallas_call(
        _gla_kernel,
        grid=(s // c,),
        in_specs=[
            pl.BlockSpec((c, kw), lambda i: (i, BIG_GLA_Q // kw)),
            pl.BlockSpec((c, kw), lambda i: (i, BIG_GLA_K // kw)),
            pl.BlockSpec((c, vw), lambda i: (i, BIG_GLA_V // vw)),
            pl.BlockSpec((c, vw), lambda i: (i, BIG_GLA_R // vw)),
            pl.BlockSpec((c, SM_W), lambda i: (i, 0)),
            pl.BlockSpec((SM_W - SM_GLR, kw), lambda i: (0, 0)),
            pl.BlockSpec((1, kw), lambda i: (0, 0)),
            pl.BlockSpec(sel.shape, lambda i: (0, 0)),
            pl.BlockSpec(masks.shape, lambda i: (0, 0, 0)),
            pl.BlockSpec((1, vw), lambda i: (0, 0)),
        ],
        out_specs=pl.BlockSpec((c, vw), lambda i: (i, 0)),
        out_shape=jax.ShapeDtypeStruct((s, vw), BF16),
        scratch_shapes=[pltpu.VMEM((GLA_HEADS, GLA_DV, GLA_DK), F32)],
        compiler_params=_params("arbitrary"),
        name="gla_chunk",
    )(big, big, big, big, small, w2_pad, gate_bias.reshape(1, kw),
      jnp.asarray(sel, BF16), jnp.asarray(masks), out_norm.reshape(1, vw))


_GDN_HALO = 16
_GDN_CW = 2 * GDN_QK_HEADS * GDN_DK + GDN_V_HEADS * GDN_DV
_GDN_SLAB_B = SM_GDN_B - SM_GLR
_GDN_SLAB_A = SM_GDN_A - SM_GLR


def _gdn_constants():
    c = CHUNK
    i = np.arange(c)[:, None]
    j = np.arange(c)[None, :]
    cum = np.concatenate([(j <= i), (j > i)], axis=0).astype(np.float32)
    tri = np.stack([(i >= j), (i > j)]).astype(np.float32)
    return cum, tri


def _gdn_gates(sm_ref, alog_ref, dtb_ref):
    slab = sm_ref[:, SM_GLR:]
    beta = _sigmoid(slab)
    g = -jnp.exp(alog_ref[...]) * _softplus(slab + dtb_ref[...])
    return beta, g


def _gdn_prep_kernel(x_ref, halo_ref, sm_ref, cw_ref, alog_ref, dtb_ref, cum_ref, tri_ref,
                     qkv_out, n_out, qk_out, xs_ref):
    c = CHUNK
    kw = GDN_QK_HEADS * GDN_DK

    halo = halo_ref[...].astype(F32)
    xs_ref[0:_GDN_HALO, :] = jnp.where(pl.program_id(0) > 0, halo, 0.0)
    xs_ref[_GDN_HALO:, :] = x_ref[...].astype(F32)
    conv = cw_ref[0:1, :] * xs_ref[pl.ds(_GDN_HALO - 3, c), :]
    for tap in range(1, GDN_CONV):
        conv = conv + cw_ref[tap:tap + 1, :] * xs_ref[pl.ds(_GDN_HALO - 3 + tap, c), :]
    qkv = _silu(conv)
    qkv_out[:, 2 * kw:] = qkv[:, 2 * kw:].astype(BF16)

    beta, g = _gdn_gates(sm_ref, alog_ref, dtb_ref)
    low = cum_ref[0:c, :]
    causal = tri_ref[0]
    strict = tri_ref[1]

    kk, qk0 = [], []
    for hq in range(GDN_QK_HEADS):
        qr = qkv[:, hq * GDN_DK:(hq + 1) * GDN_DK]
        kr = qkv[:, kw + hq * GDN_DK:kw + (hq + 1) * GDN_DK]
        qb = (qr * lax.rsqrt(jnp.sum(qr * qr, axis=-1, keepdims=True) + EPS) * (GDN_DK ** -0.5)).astype(BF16)
        kb = (kr * lax.rsqrt(jnp.sum(kr * kr, axis=-1, keepdims=True) + EPS)).astype(BF16)
        qkv_out[:, hq * GDN_DK:(hq + 1) * GDN_DK] = qb
        qkv_out[:, kw + hq * GDN_DK:kw + (hq + 1) * GDN_DK] = kb
        kk.append(_dot_nt(kb, kb))
        qk0.append(_dot_nt(qb, kb))

    rep = GDN_V_HEADS // GDN_QK_HEADS
    for hv in range(GDN_V_HEADS):
        hq = hv // rep
        bcol = beta[:, _GDN_SLAB_B + hv:_GDN_SLAB_B + hv + 1]
        d = _dot_exact_lhs(low, g[:, _GDN_SLAB_A + hv:_GDN_SLAB_A + hv + 1] * strict)
        gam = causal * jnp.exp(d)
        n_out[0, hv] = strict * (bcol * gam * kk[hq])
        qk_out[0, hv] = (gam * qk0[hq]).astype(BF16)


def gdn_prep(big, small, conv_w, alog_slab, dtb_slab):
    s = big.shape[0]
    c = CHUNK
    nc = s // c
    cum, tri = _gdn_constants()
    halo_per_chunk = c // _GDN_HALO
    slab_w = SM_W - SM_GLR
    const2 = lambda i: (0, 0)
    hcc = (GDN_V_HEADS, c, c)
    return pl.pallas_call(
        _gdn_prep_kernel,
        grid=(nc,),
        in_specs=[
            pl.BlockSpec((c, _GDN_CW), lambda i: (i, BIG_GDN_QKV // _GDN_CW)),
            pl.BlockSpec((_GDN_HALO, _GDN_CW),
                         lambda i: (jnp.maximum(i * halo_per_chunk - 1, 0), BIG_GDN_QKV // _GDN_CW)),
            pl.BlockSpec((c, SM_W), lambda i: (i, 0)),
            pl.BlockSpec((GDN_CONV, _GDN_CW), const2),
            pl.BlockSpec((1, slab_w), const2),
            pl.BlockSpec((1, slab_w), const2),
            pl.BlockSpec(cum.shape, const2),
            pl.BlockSpec(tri.shape, lambda i: (0, 0, 0)),
        ],
        out_specs=[
            pl.BlockSpec((c, _GDN_CW), lambda i: (i, 0)),
            pl.BlockSpec((1,) + hcc, lambda i: (i, 0, 0, 0)),
            pl.BlockSpec((1,) + hcc, lambda i: (i, 0, 0, 0)),
        ],
        out_shape=[
            jax.ShapeDtypeStruct((s, _GDN_CW), BF16),
            jax.ShapeDtypeStruct((nc,) + hcc, F32),
            jax.ShapeDtypeStruct((nc,) + hcc, BF16),
        ],
        scratch_shapes=[pltpu.VMEM((_GDN_HALO + c, _GDN_CW), F32)],
        compiler_params=_params("parallel"),
        name="gdn_prep",
    )(big, big, small, conv_w, alog_slab, dtb_slab, jnp.asarray(cum, BF16), jnp.asarray(tri))


def _gdn_solve_kernel(n_ref, x_ref):
    c = CHUNK
    group = 8
    x_ref[...] = jnp.zeros(x_ref.shape, F32)
    col = lax.broadcasted_iota(jnp.int32, x_ref.shape[1:], 0)

    def row(i, carry):
        def kgroup(kg, acc):
            k0 = pl.multiple_of(kg * group, group)
            n8 = n_ref[i, pl.ds(k0, group), :]
            for r in range(group):
                acc = acc - n8[r:r + 1, :] * x_ref[k0 + r]
            return acc

        acc0 = jnp.where(col == i, 1.0, 0.0)
        n_groups = lax.shift_right_logical(i + (group - 1), 3)
        x_ref[i] = lax.fori_loop(0, n_groups, kgroup, acc0)
        return carry

    lax.fori_loop(0, c, row, 0)


def gdn_solve(n_t):
    c = CHUNK
    p = n_t.shape[2]
    pb = min(128, p)
    spec = pl.BlockSpec((c, c, pb), lambda i: (0, 0, i))
    return pl.pallas_call(
        _gdn_solve_kernel,
        grid=(p // pb,),
        in_specs=[spec],
        out_specs=spec,
        out_shape=jax.ShapeDtypeStruct(n_t.shape, F32),
        compiler_params=_params("parallel"),
        name="gdn_solve",
    )(n_t)


def _gdn_kernel(qkv_ref, z_ref, sm_ref, alog_ref, dtb_ref, cum_ref, x_ref, qk_ref, gn_ref,
                y_ref, st_ref):
    c = CHUNK
    kw = GDN_QK_HEADS * GDN_DK

    @pl.when(pl.program_id(0) == 0)
    def _():
        st_ref[...] = jnp.zeros(st_ref.shape, F32)

    beta_all, g_all = _gdn_gates(sm_ref, alog_ref, dtb_ref)
    rep = GDN_V_HEADS // GDN_QK_HEADS
    for sub in range(qkv_ref.shape[0] // c):
        rows = slice(sub * c, (sub + 1) * c)
        beta = beta_all[rows]
        cums = _dot_exact_lhs(cum_ref[...], g_all[rows])
        e_g = jnp.exp(cums[0:c])
        e_gk = jnp.exp(cums[c:2 * c])
        e_last = e_g[c - 1:c, :]
        for hv in range(GDN_V_HEADS):
            hq = hv // rep
            vc = slice(hv * GDN_DV, (hv + 1) * GDN_DV)
            lane = _GDN_SLAB_A + hv
            qn = qkv_ref[rows, hq * GDN_DK:(hq + 1) * GDN_DK].astype(F32)
            kn = qkv_ref[rows, kw + hq * GDN_DK:kw + (hq + 1) * GDN_DK].astype(F32)
            v = qkv_ref[rows, 2 * kw + hv * GDN_DV:2 * kw + (hv + 1) * GDN_DV].astype(F32)
            bcol = beta[:, _GDN_SLAB_B + hv:_GDN_SLAB_B + hv + 1]
            egc = e_g[:, lane:lane + 1]
            rhs = jnp.concatenate([(bcol * egc) * kn, bcol * v], axis=1)
            wu = _dot_hp(x_ref[sub, hv], rhs)
            w = wu[:, 0:GDN_DK]
            u0 = wu[:, GDN_DK:]
            st = st_ref[hv]
            stb = st.astype(BF16)
            u = u0 - _dot(w.astype(BF16), stb)
            ub = u.astype(BF16)
            o = _dot((qn * egc).astype(BF16), stb) + _dot(qk_ref[sub, hv], ub)
            kd = (kn * e_gk[:, lane:lane + 1]).astype(BF16)
            st_ref[hv] = st * e_last[:, lane:lane + 1] + _dot_tn(kd, ub)
            o = _rms(o, gn_ref[:, vc])
            y_ref[rows, vc] = (o * _silu(z_ref[rows, vc].astype(F32))).astype(y_ref.dtype)


def gdn_chunk(big, small, conv_w, a_log, dt_bias, out_norm):
    s = big.shape[0]
    c = CHUNK
    nc = s // c
    vw = GDN_V_HEADS * GDN_DV
    cum, _ = _gdn_constants()
    slab_w = SM_W - SM_GLR
    alog_slab = jnp.zeros((1, slab_w), F32).at[0, _GDN_SLAB_A:_GDN_SLAB_A + GDN_V_HEADS].set(a_log)
    dtb_slab = jnp.zeros((1, slab_w), F32).at[0, _GDN_SLAB_A:_GDN_SLAB_A + GDN_V_HEADS].set(dt_bias)

    qkvn, nmat, qkm = gdn_prep(big, small, conv_w, alog_slab, dtb_slab)
    n_t = nmat.reshape(nc * GDN_V_HEADS, c * c).T.reshape(c, c, nc * GDN_V_HEADS)
    x_t = gdn_solve(n_t)
    xmat = x_t.reshape(c * c, nc * GDN_V_HEADS).T.reshape(nc, GDN_V_HEADS, c, c)

    const2 = lambda i: (0, 0)
    per_step = _SCAN_CHUNKS_PER_STEP
    assert nc % per_step == 0
    rows = per_step * c
    hcc = (per_step, GDN_V_HEADS, c, c)
    return pl.pallas_call(
        _gdn_kernel,
        grid=(nc // per_step,),
        in_specs=[
            pl.BlockSpec((rows, _GDN_CW), lambda i: (i, 0)),
            pl.BlockSpec((rows, vw), lambda i: (i, BIG_GDN_Z // vw)),
            pl.BlockSpec((rows, SM_W), lambda i: (i, 0)),
            pl.BlockSpec((1, slab_w), const2),
            pl.BlockSpec((1, slab_w), const2),
            pl.BlockSpec(cum.shape, const2),
            pl.BlockSpec(hcc, lambda i: (i, 0, 0, 0)),
            pl.BlockSpec(hcc, lambda i: (i, 0, 0, 0)),
            pl.BlockSpec((1, vw), const2),
        ],
        out_specs=pl.BlockSpec((rows, vw), lambda i: (i, 0)),
        out_shape=jax.ShapeDtypeStruct((s, vw), BF16),
        scratch_shapes=[pltpu.VMEM((GDN_V_HEADS, GDN_DK, GDN_DV), F32)],
        compiler_params=_params("arbitrary"),
        name="gdn_chunk",
    )(qkvn, big, small, alog_slab, dtb_slab, jnp.asarray(cum, BF16), xmat, qkm,
      out_norm.reshape(1, vw))


def _gated_mix_kernel(y0_ref, y1_ref, y2_ref, w0_ref, w1_ref, w2_ref, g0_ref, g1_ref, g2_ref, o_ref):
    acc = _sigmoid(g0_ref[...].astype(F32)) * _dot(y0_ref[...], w0_ref[...])
    acc = acc + _sigmoid(g1_ref[...].astype(F32)) * _dot(y1_ref[...], w1_ref[...])
    acc = acc + _sigmoid(g2_ref[...].astype(F32)) * _dot(y2_ref[...], w2_ref[...])
    o_ref[...] = acc.astype(o_ref.dtype)


def gated_mix(ys, ws, big, *, tm, tn):
    s, k = ys[0].shape
    n = ws[0].shape[1]
    y_spec = pl.BlockSpec((tm, k), lambda i, j: (i, 0))
    w_spec = pl.BlockSpec((k, tn), lambda i, j: (0, j))

    def gate_spec(b):
        base = (BIG_MERGE + b * n) // tn
        return pl.BlockSpec((tm, tn), lambda i, j: (i, base + j))

    return pl.pallas_call(
        _gated_mix_kernel,
        grid=(s // tm, n // tn),
        in_specs=[y_spec] * 3 + [w_spec] * 3 + [gate_spec(0), gate_spec(1), gate_spec(2)],
        out_specs=pl.BlockSpec((tm, tn), lambda i, j: (i, j)),
        out_shape=jax.ShapeDtypeStruct((s, n), BF16),
        compiler_params=_params("parallel", "parallel"),
        name="gated_mix",
    )(*ys, *ws, big, big, big)


_FFN_HALO = 8


def _ffn_up_kernel(x_ref, xh_ref, g_ref, wg_ref, wu_ref, cw_ref, cb_ref, o_ref, h_ref, hh_ref, gs_ref):
    tm = x_ref.shape[0]

    @pl.when(pl.program_id(1) == 0)
    def _():
        h_ref[...] = _rms(x_ref[...], g_ref[...]).astype(BF16)
        hh_ref[...] = _rms(xh_ref[...], g_ref[...]).astype(BF16)

    gate_halo = _dot(hh_ref[...], wg_ref[...])
    gs_ref[0:_FFN_HALO, :] = jnp.where(pl.program_id(0) > 0, gate_halo, 0.0)
    gs_ref[_FFN_HALO:, :] = _dot(h_ref[...], wg_ref[...])
    conv = cb_ref[...] + cw_ref[0:1, :] * gs_ref[pl.ds(_FFN_HALO - 2, tm), :]
    for tap in range(1, FFN_CONV):
        conv = conv + cw_ref[tap:tap + 1, :] * gs_ref[pl.ds(_FFN_HALO - 2 + tap, tm), :]
    up = _dot(h_ref[...], wu_ref[...])
    o_ref[...] = (_silu(conv) * up).astype(o_ref.dtype)


def ffn_up(x, g, w_up, conv_w, conv_b, *, tm, tn):
    s, k = x.shape
    halo_per_tile = tm // _FFN_HALO
    n_tiles = D_FF // tn
    return pl.pallas_call(
        _ffn_up_kernel,
        grid=(s // tm, n_tiles),
        in_specs=[
            pl.BlockSpec((tm, k), lambda i, j: (i, 0)),
            pl.BlockSpec((_FFN_HALO, k), lambda i, j: (jnp.maximum(i * halo_per_tile - 1, 0), 0)),
            pl.BlockSpec((1, k), lambda i, j: (0, 0)),
            pl.BlockSpec((k, tn), lambda i, j: (0, j)),
            pl.BlockSpec((k, tn), lambda i, j: (0, n_tiles + j)),
            pl.BlockSpec((FFN_CONV, tn), lambda i, j: (0, j)),
            pl.BlockSpec((1, tn), lambda i, j: (0, j)),
        ],
        out_specs=pl.BlockSpec((tm, tn), lambda i, j: (i, j)),
        out_shape=jax.ShapeDtypeStruct((s, D_FF), BF16),
        scratch_shapes=[
            pltpu.VMEM((tm, k), BF16),
            pltpu.VMEM((_FFN_HALO, k), BF16),
            pltpu.VMEM((_FFN_HALO + tm, tn), F32),
        ],
        compiler_params=_params("parallel", "arbitrary"),
        name="ffn_up",
    )(x, x, g.reshape(1, k), w_up, w_up, conv_w, conv_b.reshape(1, -1))


def _rms_kernel(x_ref, g_ref, o_ref):
    o_ref[...] = _rms(x_ref[...], g_ref[...])


def rms_only(x, g, *, tm):
    s, k = x.shape
    return pl.pallas_call(
        _rms_kernel,
        grid=(s // tm,),
        in_specs=[pl.BlockSpec((tm, k), lambda i: (i, 0)), pl.BlockSpec((1, k), lambda i: (0, 0))],
        out_specs=pl.BlockSpec((tm, k), lambda i: (i, 0)),
        out_shape=jax.ShapeDtypeStruct((s, k), F32),
        compiler_params=_params("parallel"),
        name="final_rms",
    )(x, g.reshape(1, k))


def _in_proj_columns():
    widths = (MLA_RANK, MLA_RANK, MLA_ROPE, 512, 512, 1024, GLA_GATE_RANK, 1024, 2048,
              GDN_V_HEADS, GDN_V_HEADS, 1024, 3 * D_MODEL)
    names = ("c_q", "c_kv", "k_rope", "gla_q", "gla_k", "gla_v", "gla_g", "gla_r", "gdn_qkv",
             "gdn_b", "gdn_a", "gdn_z", "merge")
    off, cols = 0, {}
    for nme, wd in zip(names, widths):
        cols[nme] = (off, off + wd)
        off += wd
    return cols


def _prep_in_proj(w):
    cols = _in_proj_columns()

    def take(name):
        a, b = cols[name]
        return w[:, a:b]

    kr = take("k_rope")
    half = MLA_ROPE // 2
    kr_rot = jnp.concatenate([-kr[:, half:], kr[:, :half]], axis=1)
    small = jnp.concatenate(
        [kr, kr_rot, take("gla_g"), take("gdn_b"), take("gdn_a"),
         jnp.zeros((w.shape[0], SM_W - SM_GDN_A - GDN_V_HEADS), w.dtype)], axis=1)
    big = jnp.concatenate([take(n).astype(BF16) for n in
                           ("c_q", "c_kv", "gla_q", "gla_k", "gla_v", "gla_r", "gdn_qkv", "gdn_z", "merge")],
                          axis=1)
    return big, small.astype(BF16)


def _prep_mla_weights(w_uq, w_ukv):
    half = MLA_ROPE // 2
    wq = w_uq.reshape(MLA_RANK, MLA_HEADS, MLA_QK)
    nope = wq[:, :, :MLA_NOPE].reshape(MLA_RANK, -1)
    rope = wq[:, :, MLA_NOPE:]
    rot = jnp.concatenate([-rope[:, :, half:], rope[:, :, :half]], axis=2)
    wq_ext = jnp.concatenate([nope, rope.reshape(MLA_RANK, -1), rot.reshape(MLA_RANK, -1)], axis=1)
    wkv = w_ukv.reshape(MLA_RANK, MLA_HEADS, MLA_NOPE + MLA_V)
    wkv_ext = jnp.concatenate([wkv[:, :, :MLA_NOPE].reshape(MLA_RANK, -1),
                               wkv[:, :, MLA_NOPE:].reshape(MLA_RANK, -1)], axis=1)
    return wq_ext.astype(BF16), wkv_ext.astype(BF16)


def _rope_tables(positions):
    inv = ROPE_THETA ** (-jnp.arange(0, MLA_ROPE, 2, dtype=F32) / MLA_ROPE)
    ang = positions.astype(F32)[0][:, None] * inv
    cos = jnp.tile(jnp.cos(ang), (1, 2 * MLA_HEADS))
    sin = jnp.tile(jnp.sin(ang), (1, 2 * MLA_HEADS))
    return cos, sin


def _tile(s, pref):
    t = pref
    while s % t:
        t //= 2
    return t


def kernel(x, positions, attn_norm, w_in, mla_q_norm, mla_kv_norm, mla_w_uq, mla_w_ukv, gla_w_gate2,
           gla_gate_bias, gla_out_norm, gdn_conv_w, gdn_a_log, gdn_dt_bias, gdn_out_norm, w_branch_mla,
           w_branch_gla, w_branch_gdn, w_out, ffn_norm, ffn_w_up, ffn_conv_w, ffn_conv_b, ffn_w_down,
           final_norm):
    b, s, d = x.shape
    assert b == 1 and d == D_MODEL and s % CHUNK == 0
    depth = w_in.shape[0]
    xs = x.reshape(s, d)
    cos8, sin8 = _rope_tables(positions)
    tm = _tile(s, 1024)
    tm_small = _tile(s, 512)
    tq = _tile(s, 1024)

    for l in range(depth):
        w_big, w_small = _prep_in_proj(w_in[l])
        big = rms_matmul(xs, attn_norm[l], w_big, out_dtype=BF16, tm=tm, tn=1024)
        small = rms_matmul(xs, attn_norm[l], w_small, out_dtype=F32, tm=tm, tn=SM_W)

        wq, wkv = _prep_mla_weights(mla_w_uq[l], mla_w_ukv[l])
        qc, kc, v = mla_prep(big, small, cos8, sin8, mla_q_norm[l], mla_kv_norm[l], wq, wkv, tm=tm_small)
        y_mla = flash_attention(qc, kc, v, tq=tq, tk=tq)
        y_gla = gla_chunk(big, small, gla_w_gate2[l], gla_gate_bias[l], gla_out_norm[l])
        y_gdn = gdn_chunk(big, small, gdn_conv_w[l], gdn_a_log[l], gdn_dt_bias[l], gdn_out_norm[l])

        mixed = gated_mix((y_mla, y_gla, y_gdn),
                          (w_branch_mla[l].astype(BF16), w_branch_gla[l].astype(BF16),
                           w_branch_gdn[l].astype(BF16)), big, tm=tm, tn=512)
        xs = matmul_residual(mixed, w_out[l].astype(BF16), xs, tm=tm, tk=512)

        act = ffn_up(xs, ffn_norm[l], ffn_w_up[l].astype(BF16), ffn_conv_w[l], ffn_conv_b[l],
                     tm=tm, tn=512)
        xs = matmul_residual(act, ffn_w_down[l].astype(BF16), xs, tm=tm, tk=512)

    out = rms_only(xs, final_norm, tm=tm_small)
    return out.reshape(b, s, d)
```
